```python
import jax
import jax.numpy as jnp
from jax import lax
import numpy as np

D_MODEL = 1024
BATCH = 8
SEQ = 2048
DEPTH = 4
DEC_BATCH = 32
DEC_SEQ = 8
PAST_LEN = 8192
PAGE_SIZE = 128

HG_WIDTH = D_MODEL // 2
HG_EXPAND = 128
HG_HEADS = HG_WIDTH // HG_EXPAND
HG_DK = HG_EXPAND
HG_DV = HG_WIDTH // HG_HEADS
HG_CHUNK = 64
LB_FLOOR = 1e-30
FOX_HEAD_DIM = 64
FOX_WIDTH = D_MODEL // 2
FOX_HEADS = FOX_WIDTH // FOX_HEAD_DIM
FOX_FORGET_BIAS = 3.0
Q_BLOCK = 128
MASK_VALUE = -1e30
N_EXPERTS = 16
N_GROUPS = 4
EXPERTS_PER_GROUP = N_EXPERTS // N_GROUPS
TOP_K = 2
D_EXPERT = D_MODEL // 2
N_MOD = 6
DEEPNORM_ALPHA = (2 * DEPTH) ** 0.25
DEEPNORM_BETA = (8 * DEPTH) ** -0.25
LN_EPS = 1e-5
RMS_EPS = 1e-6
IN_SPLITS = (HG_WIDTH, HG_WIDTH, HG_WIDTH, HG_WIDTH, FOX_WIDTH, FOX_WIDTH, FOX_WIDTH, FOX_HEADS, D_MODEL, D_MODEL)
N_IN = sum(IN_SPLITS)

kernel_name = 'hybrid_hgrn2_fox_groupmoe_step'


def _layer_norm(x, g, b):
    xf = x.astype(jnp.float32)
    xc = xf - jnp.mean(xf, axis=-1, keepdims=True)
    var = jnp.mean(xc * xc, axis=-1, keepdims=True)
    return (xc * lax.rsqrt(var + LN_EPS) * g + b).astype(x.dtype)


def _split_cols(p):
    cuts = tuple(int(i) for i in np.cumsum(np.array(IN_SPLITS))[:-1])
    return jnp.split(p, cuts, axis=-1)


def _hgrn_lower_bounds(lower_bounds):
    p = jax.nn.softmax(lower_bounds.astype(jnp.float32), axis=0)
    return jnp.cumsum(p, axis=0) - p[0:1]


def _gla_chunked(q, k, v, log_f, s0):
    bsz, seqlen, nh, _ = q.shape
    dv = v.shape[-1]
    chunk = HG_CHUNK if seqlen % HG_CHUNK == 0 else seqlen
    n_chunks = seqlen // chunk

    def to_chunks(t):
        return t.reshape(bsz, n_chunks, chunk, nh, t.shape[-1]).transpose(1, 0, 3, 2, 4)

    causal = jnp.tril(jnp.ones((chunk, chunk), dtype=bool))[:, :, None]

    def step(state, inp):
        qc, kc, vc, gc = inp
        b = jnp.cumsum(gc, axis=2)
        diff = b[:, :, :, None, :] - b[:, :, None, :, :]
        decay = jnp.where(causal, jnp.exp(jnp.minimum(diff, 0.0)), 0.0)
        attn = jnp.einsum('bhtk,bhsk,bhtsk->bhts', qc, kc, decay)
        o = jnp.einsum('bhts,bhsv->bhtv', attn, vc) + jnp.einsum('bhtk,bhkv->bhtv', qc * jnp.exp(b), state)
        b_last = b[:, :, -1:, :]
        state = (jnp.exp(b_last[:, :, 0, :])[..., None] * state
                 + jnp.einsum('bhsk,bhsv->bhkv', kc * jnp.exp(b_last - b), vc))
        return state, o

    state, o = lax.scan(step, s0, (to_chunks(q), to_chunks(k), to_chunks(v), to_chunks(log_f)))
    o = o.transpose(1, 0, 3, 2, 4).reshape(bsz, seqlen, nh, dv)
    return o, state


def _hgrn2_branch(hq, hf, hi, hg, lower_bound, norm_g, s0):
    bsz, seqlen, _ = hq.shape
    shp = (bsz, seqlen, HG_HEADS, HG_DK)
    q = jax.nn.silu(hq.astype(jnp.float32)).reshape(shp)
    z = hf.astype(jnp.float32).reshape(shp)
    lb = lower_bound.reshape(HG_HEADS, HG_DK)
    log_f = jnp.logaddexp(jnp.log(jnp.maximum(lb, LB_FLOOR)), jnp.log1p(-lb) + jax.nn.log_sigmoid(z))
    k = (1.0 - lb) * jax.nn.sigmoid(-z)
    v = hi.astype(jnp.float32).reshape(bsz, seqlen, HG_HEADS, HG_DV)
    o, s_new = _gla_chunked(q, k, v, log_f, s0.astype(jnp.float32))
    o = o * lax.rsqrt(jnp.mean(o * o, axis=-1, keepdims=True) + RMS_EPS)
    o = o.reshape(bsz, seqlen, HG_WIDTH) * norm_g * jax.nn.silu(hg.astype(jnp.float32))
    return o.astype(hq.dtype), s_new


def _fox_attention(q, k, v, log_f, q_off):
    f_cum = jnp.cumsum(log_f.astype(jnp.float32), axis=1).transpose(0, 2, 1)
    n_q = q.shape[1]
    scale = FOX_HEAD_DIM ** -0.5
    outs = []
    for s0 in range(0, n_q, Q_BLOCK):
        s1 = min(s0 + Q_BLOCK, n_q)
        kend = q_off + s1
        s = jnp.einsum('bqhd,bkhd->bhqk', q[:, s0:s1], k[:, :kend], preferred_element_type=jnp.float32) * scale
        s = s + f_cum[:, :, q_off + s0:kend, None] - f_cum[:, :, None, :kend]
        qpos = q_off + jnp.arange(s0, s1)
        kpos = jnp.arange(kend)
        s = jnp.where(kpos[None, :] <= qpos[:, None], s, MASK_VALUE)
        p = jax.nn.softmax(s, axis=-1)
        outs.append(jnp.einsum('bhqk,bkhd->bqhd', p.astype(v.dtype), v[:, :kend]))
    return jnp.concatenate(outs, axis=1)


def _mixer(h, w_in, b_fox_f, lower_bound, hg_norm_g, w_branch_a, w_branch_b, w_out, s0, past):
    bsz, seqlen, _ = h.shape
    hq, hf, hi, hg, fq, fk, fv, ff, ga, gb = _split_cols(h @ w_in)
    o_a, s_new = _hgrn2_branch(hq, hf, hi, hg, lower_bound, hg_norm_g, s0)
    head_shape = (bsz, seqlen, FOX_HEADS, FOX_HEAD_DIM)
    q, k, v = fq.reshape(head_shape), fk.reshape(head_shape), fv.reshape(head_shape)
    log_f = jax.nn.log_sigmoid(ff.astype(jnp.float32) + b_fox_f.astype(jnp.float32))
    if past is None:
        k_all, v_all, lf_all, offset = k, v, log_f, 0
    else:
        pk, pv, pl = past
        k_all = jnp.concatenate([pk.astype(k.dtype), k], axis=1)
        v_all = jnp.concatenate([pv.astype(v.dtype), v], axis=1)
        lf_all = jnp.concatenate([pl.astype(jnp.float32), log_f], axis=1)
        offset = pk.shape[1]
    o_b = _fox_attention(q, k_all, v_all, lf_all, offset).reshape(bsz, seqlen, FOX_WIDTH)
    merged = jax.nn.sigmoid(ga) * (o_a @ w_branch_a) + jax.nn.sigmoid(gb) * (o_b @ w_branch_b)
    return merged @ w_out, s_new, k, v, log_f.astype(h.dtype)


def _moe(h, w_router, b_router, w_gate, w_up, w_down):
    bsz, seqlen, d = h.shape
    t = h.reshape(bsz * seqlen, d)
    logits = jnp.matmul(t, w_router, preferred_element_type=jnp.float32) + b_router.astype(jnp.float32)
    scores = jax.nn.softmax(logits, axis=-1)
    grouped = scores.reshape(-1, N_GROUPS, EXPERTS_PER_GROUP)
    group_score = lax.top_k(grouped, TOP_K)[0].sum(axis=-1)
    group_mask = jnp.argmax(group_score, axis=-1)[:, None] == jnp.arange(N_GROUPS)[None, :]
    expert_mask = jnp.repeat(group_mask, EXPERTS_PER_GROUP, axis=-1)
    top_val, top_idx = lax.top_k(jnp.where(expert_mask, scores, -1.0), TOP_K)
    top_w = top_val / jnp.sum(top_val, axis=-1, keepdims=True)
    combine = jnp.sum(jax.nn.one_hot(top_idx, N_EXPERTS, dtype=jnp.float32) * top_w[..., None], axis=1)
    a = jnp.einsum('td,edf->tef', t, w_gate)
    u = jnp.einsum('td,edf->tef', t, w_up)
    hid = jax.nn.silu(a) * u * combine[..., None].astype(t.dtype)
    y = jnp.einsum('tef,efd->td', hid, w_down)
    return y.reshape(bsz, seqlen, d)


def _gather_past(cache_k, cache_v, cache_logf, page_table, layer):
    n_seq, n_pages = page_table.shape
    past = n_pages * PAGE_SIZE
    pages = page_table[:, :, None]
    slots = jnp.arange(PAGE_SIZE)[None, None, :]
    pk = cache_k[pages, slots, layer].reshape(n_seq, past, FOX_HEADS, FOX_HEAD_DIM)
    pv = cache_v[pages, slots, layer].reshape(n_seq, past, FOX_HEADS, FOX_HEAD_DIM)
    pl = cache_logf[pages, slots, layer].reshape(n_seq, past, FOX_HEADS)
    return pk, pv, pl


def _trunk(x, c, hg_state, paged, prm):
    bsz = x.shape[0]
    lbs = _hgrn_lower_bounds(prm['hgrn_lower_bounds'])
    x = _layer_norm(x, prm['ln_in_g'], prm['ln_in_b'])
    cond = jax.nn.silu(c)
    ks, vs, lfs, states = [], [], [], []
    for l in range(DEPTH):
        mod = (cond @ prm['w_ada'][l] + prm['b_ada'][l])[:, None, :]
        sh1, sc1, g1, sh2, sc2, g2 = jnp.split(mod, N_MOD, axis=-1)
        if hg_state is None:
            s0 = jnp.zeros((bsz, HG_HEADS, HG_DK, HG_DV), jnp.float32)
        else:
            s0 = hg_state[l]
        past = None if paged is None else _gather_past(paged[0], paged[1], paged[2], paged[3], l)
        h = x * (1 + sc1) + sh1
        m, s_new, k, v, lf = _mixer(h, prm['w_in'][l], prm['b_fox_f'][l], lbs[l], prm['hgrn_norm_g'][l],
                                    prm['w_branch_a'][l], prm['w_branch_b'][l], prm['w_out'][l], s0, past)
        x = _layer_norm(DEEPNORM_ALPHA * x + g1 * m, prm['ln1_g'][l], prm['ln1_b'][l])
        h = x * (1 + sc2) + sh2
        f = _moe(h, prm['w_router'], prm['b_router'], prm['w_exp_gate'][l], prm['w_exp_up'][l], prm['w_exp_down'][l])
        x = _layer_norm(DEEPNORM_ALPHA * x + g2 * f, prm['ln2_g'][l], prm['ln2_b'][l])
        ks.append(k)
        vs.append(v)
        lfs.append(lf)
        states.append(s_new)
    return x, jnp.stack(ks, axis=2), jnp.stack(vs, axis=2), jnp.stack(lfs, axis=2), jnp.stack(states, axis=0)


def setup_inputs(seed: int = 0) -> dict:
    key = jax.random.key(seed)
    keys = list(jax.random.split(key, 40))
    f32 = jnp.float32

    def nrm(shape, scale):
        return jax.random.normal(keys.pop(), shape, f32) * scale

    n_pages = PAST_LEN // PAGE_SIZE
    n_used = DEC_BATCH * n_pages
    n_phys = n_used + max(1, n_used // 4)
    page_table = jax.random.permutation(keys.pop(), n_phys)[:n_used].reshape(DEC_BATCH, n_pages).astype(jnp.int32)
    return {
        'x_prompt': nrm((BATCH, SEQ, D_MODEL), 1.0),
        'x_sample': nrm((DEC_BATCH, DEC_SEQ, D_MODEL), 1.0),
        'c_prompt': nrm((BATCH, D_MODEL), 1.0),
        'c_sample': nrm((DEC_BATCH, D_MODEL), 1.0),
        'cache_k': nrm((n_phys, PAGE_SIZE, DEPTH, FOX_HEADS, FOX_HEAD_DIM), 1.0),
        'cache_v': nrm((n_phys, PAGE_SIZE, DEPTH, FOX_HEADS, FOX_HEAD_DIM), 1.0),
        'cache_logf': jax.nn.log_sigmoid(FOX_FORGET_BIAS + nrm((n_phys, PAGE_SIZE, DEPTH, FOX_HEADS), 1.0)),
        'state_hgrn': nrm((DEPTH, DEC_BATCH, HG_HEADS, HG_DK, HG_DV), 1.0),
        'page_table': page_table,
        'ln_in_g': 1.0 + nrm((D_MODEL,), 0.02),
        'ln_in_b': nrm((D_MODEL,), 0.02),
        'w_ada': nrm((DEPTH, D_MODEL, N_MOD * D_MODEL), D_MODEL ** -0.5),
        'b_ada': nrm((DEPTH, N_MOD * D_MODEL), 0.02),
        'w_in': nrm((DEPTH, D_MODEL, N_IN), D_MODEL ** -0.5),
        'b_fox_f': FOX_FORGET_BIAS + nrm((DEPTH, FOX_HEADS), 0.1),
        'hgrn_lower_bounds': nrm((DEPTH, HG_WIDTH), 0.1),
        'hgrn_norm_g': 1.0 + nrm((DEPTH, HG_WIDTH), 0.02),
        'w_branch_a': nrm((DEPTH, HG_WIDTH, D_MODEL), DEEPNORM_BETA * HG_WIDTH ** -0.5),
        'w_branch_b': nrm((DEPTH, FOX_WIDTH, D_MODEL), DEEPNORM_BETA * FOX_WIDTH ** -0.5),
        'w_out': nrm((DEPTH, D_MODEL, D_MODEL), DEEPNORM_BETA * D_MODEL ** -0.5),
        'ln1_g': 1.0 + nrm((DEPTH, D_MODEL), 0.02),
        'ln1_b': nrm((DEPTH, D_MODEL), 0.02),
        'w_router': nrm((D_MODEL, N_EXPERTS), D_MODEL ** -0.5),
        'b_router': nrm((N_EXPERTS,), 0.01),
        'w_exp_gate': nrm((DEPTH, N_EXPERTS, D_MODEL, D_EXPERT), D_MODEL ** -0.5),
        'w_exp_up': nrm((DEPTH, N_EXPERTS, D_MODEL, D_EXPERT), D_MODEL ** -0.5),
        'w_exp_down': nrm((DEPTH, N_EXPERTS, D_EXPERT, D_MODEL), DEEPNORM_BETA * D_EXPERT ** -0.5),
        'ln2_g': 1.0 + nrm((DEPTH, D_MODEL), 0.02),
        'ln2_b': nrm((DEPTH, D_MODEL), 0.02),
    }


def reference(x_prompt, x_sample, c_prompt, c_sample, cache_k, cache_v, cache_logf, state_hgrn, page_table,
              ln_in_g, ln_in_b, w_ada, b_ada, w_in, b_fox_f, hgrn_lower_bounds, hgrn_norm_g,
              w_branch_a, w_branch_b, w_out, ln1_g, ln1_b, w_router, b_router,
              w_exp_gate, w_exp_up, w_exp_down, ln2_g, ln2_b):
    prm = dict(ln_in_g=ln_in_g, ln_in_b=ln_in_b, w_ada=w_ada, b_ada=b_ada, w_in=w_in, b_fox_f=b_fox_f,
               hgrn_lower_bounds=hgrn_lower_bounds, hgrn_norm_g=hgrn_norm_g, w_branch_a=w_branch_a,
               w_branch_b=w_branch_b, w_out=w_out, ln1_g=ln1_g, ln1_b=ln1_b, w_router=w_router,
               b_router=b_router, w_exp_gate=w_exp_gate, w_exp_up=w_exp_up, w_exp_down=w_exp_down,
               ln2_g=ln2_g, ln2_b=ln2_b)
    y_prompt, k_prompt, v_prompt, logf_prompt, hgrn_prompt = _trunk(x_prompt, c_prompt, None, None, prm)
    y_sample, k_sample, v_sample, logf_sample, hgrn_sample = _trunk(
        x_sample, c_sample, state_hgrn, (cache_k, cache_v, cache_logf, page_table), prm)
    hgrn_prompt = hgrn_prompt.astype(x_prompt.dtype)
    hgrn_sample = hgrn_sample.astype(state_hgrn.dtype)
    return (y_prompt, y_sample, k_prompt, v_prompt, logf_prompt, hgrn_prompt,
            k_sample, v_sample, logf_sample, hgrn_sample)
```

```python
import functools

import jax
import jax.numpy as jnp
from jax import lax
from jax.experimental import pallas as pl
from jax.experimental.pallas import tpu as pltpu

F32 = jnp.float32
BF16 = jnp.bfloat16

D_MODEL = 1024
DEPTH = 4
PAGE_SIZE = 128
HG_WIDTH = 512
HG_HEADS = 4
HG_DK = 128
HG_CHUNK = 64
HG_SUB = 16
LB_FLOOR = 1e-30
EXP_CLAMP = 80.0
FOX_HEADS = 8
FOX_HEAD_DIM = 64
FOX_WIDTH = 512
MASK_VALUE = -1e30
N_EXPERTS = 16
N_GROUPS = 4
EXPERTS_PER_GROUP = 4
D_EXPERT = 512
N_MOD = 6
DEEPNORM_ALPHA = (2 * DEPTH) ** 0.25
LN_EPS = 1e-5
RMS_EPS = 1e-6
LANES = 128
VMEM_LIMIT = 56 * 1024 * 1024

NN_DIMS = (((1,), (0,)), ((), ()))
NT_DIMS = (((1,), (1,)), ((), ()))
TN_DIMS = (((0,), (0,)), ((), ()))


def _cparams(sem):
    return pltpu.CompilerParams(dimension_semantics=sem, vmem_limit_bytes=VMEM_LIMIT)


def _dot(a, b, dims=NN_DIMS):
    return lax.dot_general(a, b, dims, preferred_element_type=F32)


def _split3(x):
    x1 = x.astype(BF16).astype(F32)
    r1 = x - x1
    x2 = r1.astype(BF16).astype(F32)
    x3 = (r1 - x2).astype(BF16).astype(F32)
    return (x1, x2, x3)


def _dot_sel_lhs(sel, b, dims=NN_DIMS):
    b1, b2, b3 = _split3(b)
    return _dot(sel, b3, dims) + _dot(sel, b2, dims) + _dot(sel, b1, dims)


def _dot_sel_rhs(a, sel, dims=NN_DIMS):
    a1, a2, a3 = _split3(a)
    return _dot(a3, sel, dims) + _dot(a2, sel, dims) + _dot(a1, sel, dims)


def _dot_f32(a, b, dims=NN_DIMS):
    a1, a2, a3 = _split3(a)
    b1, b2, b3 = _split3(b)
    small = _dot(a2, b2, dims) + _dot(a1, b3, dims) + _dot(a3, b1, dims)
    mid = _dot(a1, b2, dims) + _dot(a2, b1, dims)
    return small + mid + _dot(a1, b1, dims)


def _layer_norm(x, g, b):
    xc = x - jnp.mean(x, axis=-1, keepdims=True)
    var = jnp.mean(xc * xc, axis=-1, keepdims=True)
    return xc * lax.rsqrt(var + LN_EPS) * g + b


def _silu(x):
    return x * jax.nn.sigmoid(x)


def _log_sigmoid(x):
    return jnp.minimum(x, 0.0) - jnp.log1p(jnp.exp(-jnp.abs(x)))


def _const_spec(shape):
    nd = len(shape)
    return pl.BlockSpec(shape, lambda *_: (0,) * nd)


def _tri(n, lower):
    r = lax.broadcasted_iota(jnp.int32, (n, n), 0)
    c = lax.broadcasted_iota(jnp.int32, (n, n), 1)
    return jnp.where((r >= c) if lower else (r <= c), 1.0, 0.0).astype(F32)


def _mod_spec(mod, k, tm, tiles_per_seq):
    if mod.shape[2] == 1:
        return pl.BlockSpec((1, 1, 1, D_MODEL), lambda i: (k, i // tiles_per_seq, 0, 0))
    return pl.BlockSpec((1, 1, tm, D_MODEL), lambda i: (k, i, 0, 0))


def _ada_kernel(c_ref, w_ref, b_ref, o_ref):
    s = _silu(c_ref[...]).astype(BF16)
    o_ref[0] = _dot(s, w_ref[0].astype(BF16)) + b_ref[0]


def _ada_call(c_all, w_ada, b_ada):
    n = c_all.shape[0]
    width = N_MOD * D_MODEL
    tn = 1536
    return pl.pallas_call(
        _ada_kernel,
        grid=(DEPTH, width // tn),
        in_specs=[
            pl.BlockSpec((n, D_MODEL), lambda l, j: (0, 0)),
            pl.BlockSpec((1, D_MODEL, tn), lambda l, j: (l, 0, j)),
            pl.BlockSpec((1, 1, tn), lambda l, j: (l, 0, j)),
        ],
        out_specs=pl.BlockSpec((1, n, tn), lambda l, j: (l, 0, j)),
        out_shape=jax.ShapeDtypeStruct((DEPTH, n, width), F32),
        compiler_params=_cparams(("arbitrary", "arbitrary")),
        name="ada_mod",
    )(c_all, w_ada, b_ada.reshape(DEPTH, 1, width))


def _mixin_kernel(has_prev, prompt, n_alias, tiles_per_seq, *refs):
    it = iter(refs)
    xin_ref = next(it)
    if has_prev:
        fin_ref, g2_ref = next(it), next(it)
    lng_ref, lnb_ref, sh_ref, sc_ref = next(it), next(it), next(it), next(it)
    wh_ref, wq_ref, wk_ref, wv_ref, wff_ref, wg_ref = (next(it), next(it), next(it), next(it),
                                                       next(it), next(it))
    bff_ref, bfft_ref = next(it), next(it)
    if prompt:
        tril_ref, triu_ref = next(it), next(it)
    for _ in range(n_alias):
        next(it)
    x_ref, hq_ref, hf_ref, hi_ref, hg_ref = next(it), next(it), next(it), next(it), next(it)
    fq_ref, ga_ref, gb_ref = next(it), next(it), next(it)
    k_ref, v_ref, lf_ref = next(it), next(it), next(it)
    if prompt:
        fcol_ref, frow_ref, carry_c, carry_r = next(it), next(it), next(it), next(it)

    x = xin_ref[...]
    if has_prev:
        x = DEEPNORM_ALPHA * x + g2_ref[0, 0] * fin_ref[...]
    x = _layer_norm(x, lng_ref[...], lnb_ref[...])
    x_ref[...] = x
    h = (x * (1.0 + sc_ref[0, 0]) + sh_ref[0, 0]).astype(BF16)
    h32 = h.astype(F32)

    ph = _dot(h, wh_ref[...], NT_DIMS)
    hq_ref[...] = ph[:, 0 * HG_WIDTH:1 * HG_WIDTH]
    hf_ref[...] = ph[:, 1 * HG_WIDTH:2 * HG_WIDTH]
    hi_ref[...] = ph[:, 2 * HG_WIDTH:3 * HG_WIDTH]
    hg_ref[...] = ph[:, 3 * HG_WIDTH:4 * HG_WIDTH]
    fq_ref[...] = (_dot(h, wq_ref[...], NT_DIMS) * (FOX_HEAD_DIM ** -0.5)).astype(fq_ref.dtype)
    pg = _dot(h, wg_ref[...], NT_DIMS)
    ga_ref[...] = pg[:, 0:D_MODEL]
    gb_ref[...] = pg[:, D_MODEL:2 * D_MODEL]

    if not prompt:
        k_ref[...] = _dot(h, wk_ref[...], NT_DIMS)
        v_ref[...] = _dot(h, wv_ref[...], NT_DIMS)
        lf_ref[...] = _log_sigmoid(_dot(h32, wff_ref[...], NT_DIMS) + bff_ref[...])
    else:
        k_ref[0, 0] = _dot(wk_ref[...], h, NT_DIMS)
        v_ref[0, 0] = _dot(wv_ref[...], h, NT_DIMS)
        lf = _log_sigmoid(_dot(h32, wff_ref[...], NT_DIMS) + bff_ref[...])
        lft = _log_sigmoid(_dot(wff_ref[...], h32, NT_DIMS) + bfft_ref[...])
        lf_ref[0, 0] = lft

        @pl.when(pl.program_id(0) % tiles_per_seq == 0)
        def _():
            carry_c[...] = jnp.zeros_like(carry_c)
            carry_r[...] = jnp.zeros_like(carry_r)

        fcol = _dot_sel_lhs(tril_ref[...], lf) + carry_c[...]
        frow = _dot_sel_rhs(lft, triu_ref[...]) + carry_r[...]
        fcol_ref[...] = fcol
        frow_ref[...] = frow
        tm = lf.shape[0]
        carry_c[...] = fcol[tm - 1:tm, :]
        carry_r[...] = frow[:, tm - 1:tm]


def _mixin_call(layer, x_in, f_in, prev_mod, ln_g, ln_b, mod, w, n_seq, seq_len, tm, prompt, kv_bufs):
    T = x_in.shape[0]
    tiles_per_seq = max(seq_len // tm, 1)
    tok = lambda width: pl.BlockSpec((tm, width), lambda i: (i, 0))
    has_prev = f_in is not None

    args, specs = [x_in], [tok(D_MODEL)]
    if has_prev:
        args += [f_in, prev_mod]
        specs += [tok(D_MODEL), _mod_spec(prev_mod, 5, tm, tiles_per_seq)]
    args += [ln_g.reshape(1, D_MODEL), ln_b.reshape(1, D_MODEL), mod, mod]
    specs += [_const_spec((1, D_MODEL)), _const_spec((1, D_MODEL)),
              _mod_spec(mod, 0, tm, tiles_per_seq), _mod_spec(mod, 1, tm, tiles_per_seq)]
    wnames = ("wh", "wq", "wk", "wv", "wff", "wg", "bff", "bfft")
    args += [w[n] for n in wnames]
    specs += [_const_spec(w[n].shape) for n in wnames]
    if prompt:
        args += [_tri(tm, True), _tri(tm, False)]
        specs += [_const_spec((tm, tm)), _const_spec((tm, tm))]

    names = ["x", "hq", "hf", "hi", "hg", "fq", "ga", "gb", "k", "v", "lf"]
    widths = [D_MODEL, HG_WIDTH, HG_WIDTH, HG_WIDTH, HG_WIDTH, FOX_WIDTH, D_MODEL, D_MODEL]
    dtypes = [F32] * 5 + [BF16 if prompt else F32] + [F32] * 2
    out_shape = [jax.ShapeDtypeStruct((T, wd), dt) for wd, dt in zip(widths, dtypes)]
    out_specs = [tok(wd) for wd in widths]
    scratch = []
    aliases = {}
    if prompt:
        fm = lambda rows: pl.BlockSpec((1, 1, rows, tm),
                                       lambda i: (i // tiles_per_seq, layer, 0, i % tiles_per_seq))
        out_shape += [jax.ShapeDtypeStruct((n_seq, DEPTH, FOX_WIDTH, seq_len), F32),
                      jax.ShapeDtypeStruct((n_seq, DEPTH, FOX_WIDTH, seq_len), F32),
                      jax.ShapeDtypeStruct((n_seq, DEPTH, FOX_HEADS, seq_len), F32)]
        out_specs += [fm(FOX_WIDTH), fm(FOX_WIDTH), fm(FOX_HEADS)]
        if kv_bufs is not None:
            for j, buf in enumerate(kv_bufs):
                aliases[len(args)] = 8 + j
                args.append(buf)
                specs.append(pl.BlockSpec(memory_space=pl.ANY))
        names += ["fcol", "frow"]
        out_shape += [jax.ShapeDtypeStruct((T, FOX_HEADS), F32), jax.ShapeDtypeStruct((FOX_HEADS, T), F32)]
        out_specs += [tok(FOX_HEADS), pl.BlockSpec((FOX_HEADS, tm), lambda i: (0, i))]
        scratch = [pltpu.VMEM((1, FOX_HEADS), F32), pltpu.VMEM((FOX_HEADS, 1), F32)]
    else:
        out_shape += [jax.ShapeDtypeStruct((T, FOX_WIDTH), F32), jax.ShapeDtypeStruct((T, FOX_WIDTH), F32),
                      jax.ShapeDtypeStruct((T, FOX_HEADS), F32)]
        out_specs += [tok(FOX_WIDTH), tok(FOX_WIDTH), tok(FOX_HEADS)]

    outs = pl.pallas_call(
        functools.partial(_mixin_kernel, has_prev, prompt, len(aliases), tiles_per_seq),
        grid=(T // tm,),
        in_specs=specs,
        out_specs=out_specs,
        out_shape=out_shape,
        scratch_shapes=scratch,
        input_output_aliases=aliases,
        compiler_params=_cparams(("arbitrary",)),
        name="mixer_in",
    )(*args)
    return dict(zip(names, outs))


def _cumsum_rows(x):
    n = x.shape[0]
    row = lax.broadcasted_iota(jnp.int32, x.shape, 0)
    shift = 1
    while shift < n:
        x = x + jnp.where(row >= shift, pltpu.roll(x, shift, 0), 0.0)
        shift *= 2
    return x


def _hgrn_kernel(layer, chunk, sub, n_chunks, hq_ref, hf_ref, hi_ref, hg_ref, lb_ref, ng_ref, s0_ref,
                 o_ref, sout_ref, st_ref):
    t = pl.program_id(1)
    mm = BF16 if chunk >= 16 else F32

    @pl.when(t == 0)
    def _():
        for hd in range(HG_HEADS):
            st_ref[hd] = s0_ref[0, hd].T

    lb_all = lb_ref[...]
    e = jnp.exp(lb_all - jnp.max(lb_all, axis=0, keepdims=True))
    p = e / jnp.sum(e, axis=0, keepdims=True)
    lb = jnp.zeros((1, HG_WIDTH), F32)
    for j in range(1, layer + 1):
        lb = lb + p[j:j + 1, :]
    log_lb = jnp.log(jnp.maximum(lb, LB_FLOOR))
    log1m_lb = jnp.log1p(-lb)
    n_sub = chunk // sub
    causal = (lax.broadcasted_iota(jnp.int32, (chunk, chunk), 1)
              <= lax.broadcasted_iota(jnp.int32, (chunk, chunk), 0))

    def chunk_body(ci, carry):
        r0 = pl.multiple_of(ci * chunk, chunk)
        for hd in range(HG_HEADS):
            ls = slice(hd * HG_DK, (hd + 1) * HG_DK)
            z = hf_ref[pl.ds(r0, chunk), ls]
            q = _silu(hq_ref[pl.ds(r0, chunk), ls])
            v = hi_ref[pl.ds(r0, chunk), ls].astype(mm)
            a = log_lb[:, ls]
            b = log1m_lb[:, ls] + _log_sigmoid(z)
            log_f = jnp.maximum(a, b) + jnp.log1p(jnp.exp(-jnp.abs(a - b)))
            k = (1.0 - lb[:, ls]) * jax.nn.sigmoid(-z)
            cum = _cumsum_rows(log_f)
            blocks = []
            for bi in range(n_sub):
                rows = slice(bi * sub, (bi + 1) * sub)
                base = jnp.zeros((1, HG_DK), F32) if bi == 0 else cum[bi * sub - 1:bi * sub, :]
                k_i = (k * jnp.exp(jnp.minimum(base - cum, EXP_CLAMP))).astype(mm)
                q_i = (q[rows, :] * jnp.exp(cum[rows, :] - base)).astype(mm)
                blocks.append(_dot(q_i, k_i, NT_DIMS))
            attn = blocks[0] if n_sub == 1 else jnp.concatenate(blocks, axis=0)
            attn = jnp.where(causal, attn, 0.0).astype(mm)
            s_t = st_ref[hd]
            o = _dot(attn, v) + _dot((q * jnp.exp(cum)).astype(mm), s_t.astype(mm), NT_DIMS)
            cum_end = cum[chunk - 1:chunk, :]
            k_end = (k * jnp.exp(cum_end - cum)).astype(mm)
            st_ref[hd] = jnp.exp(cum_end) * s_t + _dot(v, k_end, TN_DIMS)
            o = o * lax.rsqrt(jnp.mean(o * o, axis=-1, keepdims=True) + RMS_EPS)
            o_ref[pl.ds(r0, chunk), ls] = o * ng_ref[:, ls] * _silu(hg_ref[pl.ds(r0, chunk), ls])
        return carry

    lax.fori_loop(0, n_chunks, chunk_body, 0)

    @pl.when(t == pl.num_programs(1) - 1)
    def _():
        for hd in range(HG_HEADS):
            sout_ref[0, hd] = st_ref[hd].T


def _hgrn_call(layer, p, lower_bounds, norm_g, s0, n_seq, seq_len):
    chunk = HG_CHUNK if seq_len % HG_CHUNK == 0 else seq_len
    sub = min(HG_SUB, chunk)
    tb = min(seq_len, 512)
    nt = seq_len // tb
    tok = pl.BlockSpec((tb, HG_WIDTH), lambda b, t: (b * nt + t, 0))
    st_spec = pl.BlockSpec((1, HG_HEADS, HG_DK, HG_DK), lambda b, t: (b, 0, 0, 0))
    return pl.pallas_call(
        functools.partial(_hgrn_kernel, layer, chunk, sub, tb // chunk),
        grid=(n_seq, nt),
        in_specs=[tok, tok, tok, tok,
                  pl.BlockSpec((DEPTH, HG_WIDTH), lambda b, t: (0, 0)),
                  pl.BlockSpec((1, HG_WIDTH), lambda b, t: (0, 0)),
                  st_spec],
        out_specs=[tok, st_spec],
        out_shape=[jax.ShapeDtypeStruct((n_seq * seq_len, HG_WIDTH), F32),
                   jax.ShapeDtypeStruct((n_seq, HG_HEADS, HG_DK, HG_DK), F32)],
        scratch_shapes=[pltpu.VMEM((HG_HEADS, HG_DK, HG_DK), F32)],
        compiler_params=_cparams(("arbitrary", "arbitrary")),
        name="hgrn",
    )(p["hq"], p["hf"], p["hi"], p["hg"], lower_bounds, norm_g.reshape(1, HG_WIDTH), s0)


def _fox_kernel(tq, tk, q_ref, k_ref, v_ref, fcol_ref, frow_ref, o_ref, m_ref, l_ref, acc_ref):
    qi = pl.program_id(1)
    ki = pl.program_id(2)

    @pl.when(ki == 0)
    def _():
        m_ref[...] = jnp.full_like(m_ref, MASK_VALUE)
        l_ref[...] = jnp.zeros_like(l_ref)
        acc_ref[...] = jnp.zeros_like(acc_ref)

    @pl.when(ki <= qi)
    def _():
        low = lax.broadcasted_iota(jnp.int32, (1, LANES), 1) < FOX_HEAD_DIM
        qpos = qi * tq + lax.broadcasted_iota(jnp.int32, (tq, tk), 0)
        kpos = ki * tk + lax.broadcasted_iota(jnp.int32, (tq, tk), 1)
        visible = kpos <= qpos
        for pr in range(FOX_HEADS // 2):
            ls = slice(pr * LANES, (pr + 1) * LANES)
            q2 = q_ref[:, ls]
            k2 = k_ref[0, 0, ls, :].astype(BF16)
            v2 = v_ref[0, 0, ls, :].astype(BF16)
            acc2 = acc_ref[:, ls]
            new_acc = []
            for half in range(2):
                hd = 2 * pr + half
                keep = low if half == 0 else jnp.logical_not(low)
                qh = jnp.where(keep, q2, jnp.zeros_like(q2))
                s = _dot(qh, k2)
                s = s + fcol_ref[:, hd:hd + 1] - frow_ref[hd:hd + 1, :]
                s = jnp.where(visible, s, MASK_VALUE)
                m_old = m_ref[hd]
                m_new = jnp.maximum(m_old, jnp.max(s, axis=-1, keepdims=True))
                alpha = jnp.exp(m_old - m_new)
                pe = jnp.exp(s - m_new)
                l_ref[hd] = alpha * l_ref[hd] + jnp.sum(pe, axis=-1, keepdims=True)
                m_ref[hd] = m_new
                new_acc.append(alpha * acc2 + _dot(pe.astype(BF16), v2, NT_DIMS))
            acc_ref[:, ls] = jnp.where(low, new_acc[0], new_acc[1])

    @pl.when(ki == qi)
    def _():
        low = lax.broadcasted_iota(jnp.int32, (1, LANES), 1) < FOX_HEAD_DIM
        for pr in range(FOX_HEADS // 2):
            ls = slice(pr * LANES, (pr + 1) * LANES)
            inv = jnp.where(low, 1.0 / l_ref[2 * pr], 1.0 / l_ref[2 * pr + 1])
            o_ref[:, ls] = acc_ref[:, ls] * inv


def _fox_call(layer, p, n_seq, seq_len, tq):
    tk = tq
    nq = seq_len // tq
    T = n_seq * seq_len
    qspec = pl.BlockSpec((tq, FOX_WIDTH), lambda b, qi, ki: (b * nq + qi, 0))
    kspec = pl.BlockSpec((1, 1, FOX_WIDTH, tk), lambda b, qi, ki: (b, layer, 0, jnp.minimum(ki, qi)))
    return pl.pallas_call(
        functools.partial(_fox_kernel, tq, tk),
        grid=(n_seq, nq, nq),
        in_specs=[qspec, kspec, kspec,
                  pl.BlockSpec((tq, FOX_HEADS), lambda b, qi, ki: (b * nq + qi, 0)),
                  pl.BlockSpec((FOX_HEADS, tk), lambda b, qi, ki: (0, b * nq + jnp.minimum(ki, qi)))],
        out_specs=qspec,
        out_shape=jax.ShapeDtypeStruct((T, FOX_WIDTH), F32),
        scratch_shapes=[pltpu.VMEM((FOX_HEADS, tq, 1), F32), pltpu.VMEM((FOX_HEADS, tq, 1), F32),
                        pltpu.VMEM((tq, FOX_WIDTH), F32)],
        compiler_params=_cparams(("arbitrary", "arbitrary", "arbitrary")),
        name="fox_prompt",
    )(p["fq"], p["k"], p["v"], p["fcol"], p["frow"])


PAGES_PER_STEP = 16


def _decode_kernel(n_new, pt_ref, q_ref, kn_ref, vn_ref, lfn_ref, *refs):
    npg = PAGES_PER_STEP
    k_refs = refs[0:npg]
    v_refs = refs[npg:2 * npg]
    lf_refs = refs[2 * npg:3 * npg]
    o_ref, m_ref, l_ref, acc_ref, carry_ref = refs[3 * npg:]
    j = pl.program_id(1)
    rows = FOX_HEADS * n_new

    @pl.when(j == 0)
    def _():
        m_ref[...] = jnp.full_like(m_ref, MASK_VALUE)
        l_ref[...] = jnp.zeros_like(l_ref)
        acc_ref[...] = jnp.zeros_like(acc_ref)
        carry_ref[...] = jnp.zeros_like(carry_ref)

    q = q_ref[...]
    rr = lax.broadcasted_iota(jnp.int32, (rows, FOX_WIDTH), 0)
    cc = lax.broadcasted_iota(jnp.int32, (rows, FOX_WIDTH), 1)
    q_rep = jnp.concatenate([q] * FOX_HEADS, axis=0)
    qbd = jnp.where((rr // n_new) == (cc // FOX_HEAD_DIM), q_rep, 0.0)
    qbd_bf = qbd.astype(BF16)
    ehe = jnp.where(lax.broadcasted_iota(jnp.int32, (rows, FOX_HEADS), 0) // n_new
                    == lax.broadcasted_iota(jnp.int32, (rows, FOX_HEADS), 1), 1.0, 0.0)
    triu = _tri(PAGE_SIZE, False)

    def absorb(s, pv_fn):
        m_old = m_ref[...]
        m_new = jnp.maximum(m_old, jnp.max(s, axis=-1, keepdims=True))
        alpha = jnp.exp(m_old - m_new)
        pe = jnp.exp(s - m_new)
        l_ref[...] = alpha * l_ref[...] + jnp.sum(pe, axis=-1, keepdims=True)
        m_ref[...] = m_new
        acc_ref[...] = alpha * acc_ref[...] + pv_fn(pe)

    for pg in range(npg):
        kt = k_refs[pg][...].astype(BF16)
        vt = v_refs[pg][...].astype(BF16)
        f_page = _dot_sel_rhs(lf_refs[pg][...], triu) + carry_ref[...]
        carry_ref[...] = f_page[:, PAGE_SIZE - 1:PAGE_SIZE]
        s = _dot(qbd_bf, kt) - _dot_sel_lhs(ehe, f_page)
        absorb(s, lambda pe, vt=vt: _dot(pe.astype(BF16), vt, NT_DIMS))

    @pl.when(j == pl.num_programs(1) - 1)
    def _():
        kn = kn_ref[...]
        vn = vn_ref[...]
        eye = (lax.broadcasted_iota(jnp.int32, (FOX_HEADS, FOX_HEADS), 0)
               == lax.broadcasted_iota(jnp.int32, (FOX_HEADS, FOX_HEADS), 1))
        carry_row = jnp.sum(jnp.where(eye, carry_ref[...], 0.0), axis=0, keepdims=True)
        f_new = _dot_sel_lhs(_tri(n_new, True), lfn_ref[...]) + carry_row
        s = _dot(qbd, kn, NT_DIMS) - _dot_sel_lhs(ehe, f_new, NT_DIMS)
        key_i = lax.broadcasted_iota(jnp.int32, (rows, n_new), 1)
        qry_i = lax.broadcasted_iota(jnp.int32, (rows, n_new), 0) % n_new
        s = jnp.where(key_i <= qry_i, s, MASK_VALUE)
        absorb(s, lambda pe: _dot(pe, vn))
        out = acc_ref[...] / l_ref[...]
        lane_head = lax.broadcasted_iota(jnp.int32, (n_new, FOX_WIDTH), 1) // FOX_HEAD_DIM
        res = jnp.zeros((n_new, FOX_WIDTH), F32)
        for hd in range(FOX_HEADS):
            res = res + jnp.where(lane_head == hd, out[hd * n_new:(hd + 1) * n_new, :], 0.0)
        o_ref[...] = res


def _decode_call(layer, p, cache_kt, cache_vt, cache_lft, page_table, n_seq, n_new):
    n_pages = page_table.shape[1]
    npg = PAGES_PER_STEP
    pt_flat = page_table.reshape(-1).astype(jnp.int32)

    def page_map(pg):
        return lambda n, j, pt: (pt[n * n_pages + j * npg + pg], layer, 0, 0)

    new_w = pl.BlockSpec((n_new, FOX_WIDTH), lambda n, j, pt: (n, 0))
    in_specs = [new_w, new_w, new_w, pl.BlockSpec((n_new, FOX_HEADS), lambda n, j, pt: (n, 0))]
    in_specs += [pl.BlockSpec((None, None, FOX_WIDTH, PAGE_SIZE), page_map(pg)) for pg in range(npg)]
    in_specs += [pl.BlockSpec((None, None, FOX_WIDTH, PAGE_SIZE), page_map(pg)) for pg in range(npg)]
    in_specs += [pl.BlockSpec((None, None, FOX_HEADS, PAGE_SIZE), page_map(pg)) for pg in range(npg)]
    rows = FOX_HEADS * n_new
    grid_spec = pltpu.PrefetchScalarGridSpec(
        num_scalar_prefetch=1,
        grid=(n_seq, n_pages // npg),
        in_specs=in_specs,
        out_specs=new_w,
        scratch_shapes=[pltpu.VMEM((rows, 1), F32), pltpu.VMEM((rows, 1), F32),
                        pltpu.VMEM((rows, FOX_WIDTH), F32), pltpu.VMEM((FOX_HEADS, 1), F32)],
    )
    return pl.pallas_call(
        functools.partial(_decode_kernel, n_new),
        grid_spec=grid_spec,
        out_shape=jax.ShapeDtypeStruct((n_seq * n_new, FOX_WIDTH), F32),
        compiler_params=_cparams(("arbitrary", "arbitrary")),
        name="fox_decode",
    )(pt_flat, p["fq"], p["k"], p["v"], p["lf"],
      *([cache_kt] * npg), *([cache_vt] * npg), *([cache_lft] * npg))


def _merge_kernel(tiles_per_win, oa_ref, ob_ref, ga_ref, gb_ref, x_ref, g1_ref, sh_ref, sc_ref,
                  wa_ref, wb_ref, wo_ref, lng_ref, lnb_ref, wrt_ref, br_ref, triu_ref,
                  x1_ref, h2_ref, cw_ref, route_ref, carry_ref):
    ya = _dot(oa_ref[...].astype(BF16), wa_ref[...])
    yb = _dot(ob_ref[...].astype(BF16), wb_ref[...])
    merged = jax.nn.sigmoid(ga_ref[...]) * ya + jax.nn.sigmoid(gb_ref[...]) * yb
    m = _dot(merged.astype(BF16), wo_ref[...])
    x1 = _layer_norm(DEEPNORM_ALPHA * x_ref[...] + g1_ref[0, 0] * m, lng_ref[...], lnb_ref[...])
    x1_ref[...] = x1
    h2 = x1 * (1.0 + sc_ref[0, 0]) + sh_ref[0, 0]
    h2_ref[...] = h2.astype(BF16)

    tm = h2.shape[0]
    logits = _dot_f32(wrt_ref[...], h2, NT_DIMS) + br_ref[...]
    ex = jnp.exp(logits - jnp.max(logits, axis=0, keepdims=True))
    scores = ex / jnp.sum(ex, axis=0, keepdims=True)
    gs = []
    for g in range(N_GROUPS):
        r = [scores[g * EXPERTS_PER_GROUP + e:g * EXPERTS_PER_GROUP + e + 1, :]
             for e in range(EXPERTS_PER_GROUP)]
        best = r[0] + r[1]
        for a in range(EXPERTS_PER_GROUP):
            for b in range(a + 1, EXPERTS_PER_GROUP):
                best = jnp.maximum(best, r[a] + r[b])
        gs.append(best)
    gmax = jnp.maximum(jnp.maximum(gs[0], gs[1]), jnp.maximum(gs[2], gs[3]))
    gid = jnp.where(gs[0] == gmax, 0, jnp.where(gs[1] == gmax, 1, jnp.where(gs[2] == gmax, 2, 3)))
    erow = lax.broadcasted_iota(jnp.int32, (N_EXPERTS, tm), 0)
    masked = jnp.where(erow // EXPERTS_PER_GROUP == gid, scores, -1.0)
    top1 = jnp.max(masked, axis=0, keepdims=True)
    idx1 = jnp.min(jnp.where(masked == top1, erow, N_EXPERTS), axis=0, keepdims=True)
    masked2 = jnp.where(erow == idx1, -2.0, masked)
    top2 = jnp.max(masked2, axis=0, keepdims=True)
    idx2 = jnp.min(jnp.where(masked2 == top2, erow, N_EXPERTS), axis=0, keepdims=True)
    den = top1 + top2
    cw_ref[...] = jnp.where(erow == idx1, top1 / den, 0.0) + jnp.where(erow == idx2, top2 / den, 0.0)

    @pl.when(pl.program_id(0) % tiles_per_win == 0)
    def _():
        carry_ref[...] = jnp.zeros_like(carry_ref)

    grow = lax.broadcasted_iota(jnp.int32, (8, tm), 0)
    member = grow == gid
    incl = _dot(jnp.where(member, 1.0, 0.0), triu_ref[...]) + carry_ref[...]
    carry_ref[...] = incl[:, tm - 1:tm]
    rank = jnp.sum(jnp.where(member, incl, 0.0), axis=0, keepdims=True) - 1.0
    route_ref[...] = jnp.where(grow == 0, gid, jnp.where(grow == 1, rank.astype(jnp.int32), 0))


def _merge_call(o_a, o_b, p, mod, w, ln_g, ln_b, w_router_t, b_router, seq_len, tm, win):
    T = o_a.shape[0]
    tiles_per_seq = max(seq_len // tm, 1)
    tok = lambda width: pl.BlockSpec((tm, width), lambda i: (i, 0))
    return pl.pallas_call(
        functools.partial(_merge_kernel, win // tm),
        grid=(T // tm,),
        in_specs=[tok(HG_WIDTH), tok(FOX_WIDTH), tok(D_MODEL), tok(D_MODEL), tok(D_MODEL),
                  _mod_spec(mod, 2, tm, tiles_per_seq), _mod_spec(mod, 3, tm, tiles_per_seq),
                  _mod_spec(mod, 4, tm, tiles_per_seq),
                  _const_spec((HG_WIDTH, D_MODEL)), _const_spec((FOX_WIDTH, D_MODEL)),
                  _const_spec((D_MODEL, D_MODEL)), _const_spec((1, D_MODEL)), _const_spec((1, D_MODEL)),
                  _const_spec((N_EXPERTS, D_MODEL)), _const_spec((N_EXPERTS, 1)), _const_spec((tm, tm))],
        out_specs=[tok(D_MODEL), tok(D_MODEL),
                   pl.BlockSpec((N_EXPERTS, tm), lambda i: (0, i)),
                   pl.BlockSpec((8, tm), lambda i: (0, i))],
        out_shape=[jax.ShapeDtypeStruct((T, D_MODEL), F32), jax.ShapeDtypeStruct((T, D_MODEL), BF16),
                   jax.ShapeDtypeStruct((N_EXPERTS, T), F32), jax.ShapeDtypeStruct((8, T), jnp.int32)],
        scratch_shapes=[pltpu.VMEM((8, 1), F32)],
        compiler_params=_cparams(("arbitrary",)),
        name="merge",
    )(o_a, o_b, p["ga"], p["gb"], p["x"], mod, mod, mod, w["wa"], w["wb"], w["wo"],
      ln_g.reshape(1, D_MODEL), ln_b.reshape(1, D_MODEL), w_router_t, b_router.reshape(N_EXPERTS, 1),
      _tri(tm, False))


MOE_ROWS = 128


def _moe_kernel(h_ref, cw_ref, route_ref, wg_ref, wu_ref, wd_ref, y_ref):
    g = pl.program_id(1)
    win = h_ref.shape[0]

    @pl.when(g == 0)
    def _():
        y_ref[...] = jnp.zeros_like(y_ref)

    gid = route_ref[0:1, :]
    rank = route_ref[1:2, :]
    member = gid == g
    count = jnp.sum(jnp.where(member, 1, 0))
    n_steps = (count + MOE_ROWS - 1) // MOE_ROWS
    slot = lax.broadcasted_iota(jnp.int32, (MOE_ROWS, win), 0)
    ecol = lax.broadcasted_iota(jnp.int32, (MOE_ROWS, N_EXPERTS), 1)

    def body(j, carry):
        hit = jnp.logical_and(member, rank - j * MOE_ROWS == slot)
        perm_f = jnp.where(hit, 1.0, 0.0)
        perm = perm_f.astype(BF16)
        hs = _dot(perm, h_ref[...]).astype(BF16)
        cwg = _dot_sel_lhs(perm_f, cw_ref[...], NT_DIMS)
        acc = jnp.zeros((MOE_ROWS, D_MODEL), F32)
        for e in range(EXPERTS_PER_GROUP):
            cwe = jnp.sum(jnp.where(ecol == g * EXPERTS_PER_GROUP + e, cwg, 0.0), axis=1, keepdims=True)
            a = _dot(hs, wg_ref[0, e])
            u = _dot(hs, wu_ref[0, e])
            hid = (_silu(a) * u * cwe).astype(BF16)
            acc = acc + _dot(hid, wd_ref[0, e])
        y_ref[...] += _dot(perm, acc.astype(BF16), TN_DIMS)
        return carry

    lax.fori_loop(0, n_steps, body, 0)


def _moe_call(h2, cw, route, w, win):
    T = h2.shape[0]
    wspec = lambda shp: pl.BlockSpec((1,) + shp, lambda i, g: (g, 0, 0, 0))
    return pl.pallas_call(
        _moe_kernel,
        grid=(T // win, N_GROUPS),
        in_specs=[pl.BlockSpec((win, D_MODEL), lambda i, g: (i, 0)),
                  pl.BlockSpec((N_EXPERTS, win), lambda i, g: (0, i)),
                  pl.BlockSpec((8, win), lambda i, g: (0, i)),
                  wspec((EXPERTS_PER_GROUP, D_MODEL, D_EXPERT)),
                  wspec((EXPERTS_PER_GROUP, D_MODEL, D_EXPERT)),
                  wspec((EXPERTS_PER_GROUP, D_EXPERT, D_MODEL))],
        out_specs=pl.BlockSpec((win, D_MODEL), lambda i, g: (i, 0)),
        out_shape=jax.ShapeDtypeStruct((T, D_MODEL), F32),
        compiler_params=_cparams(("arbitrary", "arbitrary")),
        name="moe",
    )(h2, cw, route, w["eg"], w["eu"], w["ed"])


def _final_kernel(x_ref, f_ref, g2_ref, lng_ref, lnb_ref, o_ref):
    o_ref[...] = _layer_norm(DEEPNORM_ALPHA * x_ref[...] + g2_ref[0, 0] * f_ref[...],
                             lng_ref[...], lnb_ref[...])


def _final_call(x1, f, mod, ln_g, ln_b, seq_len, tm):
    T = x1.shape[0]
    tok = pl.BlockSpec((tm, D_MODEL), lambda i: (i, 0))
    return pl.pallas_call(
        _final_kernel,
        grid=(T // tm,),
        in_specs=[tok, tok, _mod_spec(mod, 5, tm, max(seq_len // tm, 1)),
                  _const_spec((1, D_MODEL)), _const_spec((1, D_MODEL))],
        out_specs=tok,
        out_shape=jax.ShapeDtypeStruct((T, D_MODEL), F32),
        compiler_params=_cparams(("arbitrary",)),
        name="final_norm",
    )(x1, f, mod, ln_g.reshape(1, D_MODEL), ln_b.reshape(1, D_MODEL))


def _layer_weights(prm, w_in_t, l):
    c0 = 4 * HG_WIDTH
    c1 = c0 + FOX_WIDTH
    c2 = c1 + FOX_WIDTH
    c3 = c2 + FOX_WIDTH
    c4 = c3 + FOX_HEADS
    wt = w_in_t[l]
    grouped = lambda a, shp: a.astype(BF16).reshape((N_GROUPS, EXPERTS_PER_GROUP) + shp)
    return {
        "wh": wt[:c0].astype(BF16),
        "wq": wt[c0:c1].astype(BF16),
        "wk": wt[c1:c2].astype(BF16),
        "wv": wt[c2:c3].astype(BF16),
        "wff": wt[c3:c4].astype(BF16).astype(F32),
        "wg": wt[c4:].astype(BF16),
        "bff": prm["b_fox_f"][l].reshape(1, FOX_HEADS),
        "bfft": prm["b_fox_f"][l].reshape(FOX_HEADS, 1),
        "wa": prm["w_branch_a"][l].astype(BF16),
        "wb": prm["w_branch_b"][l].astype(BF16),
        "wo": prm["w_out"][l].astype(BF16),
        "eg": grouped(prm["w_exp_gate"][l], (D_MODEL, D_EXPERT)),
        "eu": grouped(prm["w_exp_up"][l], (D_MODEL, D_EXPERT)),
        "ed": grouped(prm["w_exp_down"][l], (D_EXPERT, D_MODEL)),
    }


def _trunk(x, mods, hg_state, paged, prm, weights):
    n_seq, seq_len, _ = x.shape
    T = n_seq * seq_len
    prompt = paged is None
    tm = min(256, T)
    win = min(1024, T)
    w_router_t = prm["w_router"].T
    if prompt:
        s0_all = jnp.zeros((DEPTH, n_seq, HG_HEADS, HG_DK, HG_DK), F32)
    else:
        s0_all = hg_state.astype(F32)
        cache_k, cache_v, cache_lf, page_table = paged
        n_phys = cache_k.shape[0]
        cache_kt = jnp.transpose(cache_k, (0, 2, 3, 4, 1)).reshape(n_phys, DEPTH, FOX_WIDTH, PAGE_SIZE)
        cache_vt = jnp.transpose(cache_v, (0, 2, 3, 4, 1)).reshape(n_phys, DEPTH, FOX_WIDTH, PAGE_SIZE)
        cache_lft = jnp.transpose(cache_lf, (0, 2, 3, 1))

    ks, vs, lfs, states = [], [], [], []
    kv_bufs = None
    x_in, f_in = x.reshape(T, D_MODEL), None
    for l in range(DEPTH):
        w = weights[l]
        if l == 0:
            ln_g, ln_b, prev_mod = prm["ln_in_g"], prm["ln_in_b"], None
        else:
            ln_g, ln_b, prev_mod = prm["ln2_g"][l - 1], prm["ln2_b"][l - 1], mods[l - 1]
        p = _mixin_call(l, x_in, f_in, prev_mod, ln_g, ln_b, mods[l], w, n_seq, seq_len, tm, prompt, kv_bufs)
        o_a, s_new = _hgrn_call(l, p, prm["hgrn_lower_bounds"], prm["hgrn_norm_g"][l], s0_all[l],
                                n_seq, seq_len)
        if prompt:
            kv_bufs = (p["k"], p["v"], p["lf"])
            o_b = _fox_call(l, p, n_seq, seq_len, min(512, seq_len))
        else:
            o_b = _decode_call(l, p, cache_kt, cache_vt, cache_lft, page_table, n_seq, seq_len)
            ks.append(p["k"])
            vs.append(p["v"])
            lfs.append(p["lf"])
        x1, h2, cw, route = _merge_call(o_a, o_b, p, mods[l], w, prm["ln1_g"][l], prm["ln1_b"][l],
                                        w_router_t, prm["b_router"], seq_len, tm, win)
        f = _moe_call(h2, cw, route, w, win)
        x_in, f_in = x1, f
        states.append(s_new)
    y = _final_call(x_in, f_in, mods[DEPTH - 1], prm["ln2_g"][DEPTH - 1], prm["ln2_b"][DEPTH - 1], seq_len, tm)
    y = y.reshape(n_seq, seq_len, D_MODEL)
    if prompt:
        kb, vb, lfb = kv_bufs
        k_out = jnp.transpose(kb.reshape(n_seq, DEPTH, FOX_HEADS, FOX_HEAD_DIM, seq_len), (0, 4, 1, 2, 3))
        v_out = jnp.transpose(vb.reshape(n_seq, DEPTH, FOX_HEADS, FOX_HEAD_DIM, seq_len), (0, 4, 1, 2, 3))
        lf_out = jnp.transpose(lfb, (0, 3, 1, 2))
    else:
        k_out = jnp.stack(ks, axis=1).reshape(n_seq, seq_len, DEPTH, FOX_HEADS, FOX_HEAD_DIM)
        v_out = jnp.stack(vs, axis=1).reshape(n_seq, seq_len, DEPTH, FOX_HEADS, FOX_HEAD_DIM)
        lf_out = jnp.stack(lfs, axis=1).reshape(n_seq, seq_len, DEPTH, FOX_HEADS)
    return y, k_out, v_out, lf_out, jnp.stack(states, axis=0)


def kernel(x_prompt, x_sample, c_prompt, c_sample, cache_k, cache_v, cache_logf, state_hgrn, page_table,
           ln_in_g, ln_in_b, w_ada, b_ada, w_in, b_fox_f, hgrn_lower_bounds, hgrn_norm_g,
           w_branch_a, w_branch_b, w_out, ln1_g, ln1_b, w_router, b_router,
           w_exp_gate, w_exp_up, w_exp_down, ln2_g, ln2_b):
    prm = dict(ln_in_g=ln_in_g, ln_in_b=ln_in_b, b_fox_f=b_fox_f,
               hgrn_lower_bounds=hgrn_lower_bounds, hgrn_norm_g=hgrn_norm_g, w_branch_a=w_branch_a,
               w_branch_b=w_branch_b, w_out=w_out, ln1_g=ln1_g, ln1_b=ln1_b, w_router=w_router,
               b_router=b_router, w_exp_gate=w_exp_gate, w_exp_up=w_exp_up, w_exp_down=w_exp_down,
               ln2_g=ln2_g, ln2_b=ln2_b)
    n_p, n_s = x_prompt.shape[0], x_sample.shape[0]
    dec_seq = x_sample.shape[1]
    mod_all = _ada_call(jnp.concatenate([c_prompt, c_sample], axis=0), w_ada, b_ada)
    mods_p, mods_s = [], []
    for l in range(DEPTH):
        mp = mod_all[l, :n_p].reshape(n_p, N_MOD, D_MODEL).transpose(1, 0, 2)
        mods_p.append(mp[:, :, None, :])
        ms = mod_all[l, n_p:].reshape(n_s, N_MOD, D_MODEL).transpose(1, 0, 2)
        mods_s.append(jnp.repeat(ms, dec_seq, axis=1)[:, None, :, :])
    w_in_t = jnp.transpose(w_in, (0, 2, 1))
    weights = [_layer_weights(prm, w_in_t, l) for l in range(DEPTH)]

    y_p, k_p, v_p, lf_p, hg_p = _trunk(x_prompt, mods_p, None, None, prm, weights)
    y_s, k_s, v_s, lf_s, hg_s = _trunk(x_sample, mods_s, state_hgrn,
                                       (cache_k, cache_v, cache_logf, page_table), prm, weights)
    return (y_p, y_s, k_p, v_p, lf_p, hg_p.astype(x_prompt.dtype),
            k_s, v_s, lf_s, hg_s.astype(state_hgrn.dtype))
```

```python
import functools

import jax
import jax.numpy as jnp
import numpy as np
from jax import lax
from jax.experimental import pallas as pl
from jax.experimental.pallas import tpu as pltpu

F32 = jnp.float32
BF16 = jnp.bfloat16

D_MODEL = 1024
DEPTH = 4
PAGE_SIZE = 128
HG_WIDTH = 512
HG_HEADS = 4
HG_DK = 128
HG_CHUNK = 64
HG_SUB = 16
LB_FLOOR = 1e-30
EXP_CLAMP = 80.0
FOX_HEADS = 8
FOX_HEAD_DIM = 64
FOX_WIDTH = 512
MASK_VALUE = -1e30
N_EXPERTS = 16
N_GROUPS = 4
EXPERTS_PER_GROUP = 4
D_EXPERT = 512
N_MOD = 6
DEEPNORM_ALPHA = (2 * DEPTH) ** 0.25
LN_EPS = 1e-5
RMS_EPS = 1e-6
LANES = 128
VMEM_LIMIT = 56 * 1024 * 1024

NN_DIMS = (((1,), (0,)), ((), ()))
NT_DIMS = (((1,), (1,)), ((), ()))
TN_DIMS = (((0,), (0,)), ((), ()))


def _cparams(sem):
    return pltpu.CompilerParams(dimension_semantics=sem, vmem_limit_bytes=VMEM_LIMIT)


def _dot(a, b, dims=NN_DIMS):
    return lax.dot_general(a, b, dims, preferred_element_type=F32)


def _split3(x):
    x1 = x.astype(BF16).astype(F32)
    r1 = x - x1
    x2 = r1.astype(BF16).astype(F32)
    x3 = (r1 - x2).astype(BF16).astype(F32)
    return (x1, x2, x3)


def _dot_sel_lhs(sel, b, dims=NN_DIMS):
    b1, b2, b3 = _split3(b)
    return _dot(sel, b3, dims) + _dot(sel, b2, dims) + _dot(sel, b1, dims)


def _dot_sel_rhs(a, sel, dims=NN_DIMS):
    a1, a2, a3 = _split3(a)
    return _dot(a3, sel, dims) + _dot(a2, sel, dims) + _dot(a1, sel, dims)


def _dot_f32(a, b, dims=NN_DIMS):
    a1, a2, a3 = _split3(a)
    b1, b2, b3 = _split3(b)
    small = _dot(a2, b2, dims) + _dot(a1, b3, dims) + _dot(a3, b1, dims)
    mid = _dot(a1, b2, dims) + _dot(a2, b1, dims)
    return small + mid + _dot(a1, b1, dims)


def _layer_norm(x, g, b):
    xc = x - jnp.mean(x, axis=-1, keepdims=True)
    var = jnp.mean(xc * xc, axis=-1, keepdims=True)
    return xc * lax.rsqrt(var + LN_EPS) * g + b


def _silu(x):
    return x * jax.nn.sigmoid(x)


def _log_sigmoid(x):
    return jnp.minimum(x, 0.0) - jnp.log1p(jnp.exp(-jnp.abs(x)))


def _const_spec(shape):
    nd = len(shape)
    return pl.BlockSpec(shape, lambda *_: (0,) * nd)


def _tri(n, lower):
    r = lax.broadcasted_iota(jnp.int32, (n, n), 0)
    c = lax.broadcasted_iota(jnp.int32, (n, n), 1)
    return jnp.where((r >= c) if lower else (r <= c), 1.0, 0.0).astype(F32)


def _mod_spec(mod, k, tm, tiles_per_seq):
    if mod.shape[2] == 1:
        return pl.BlockSpec((1, 1, 1, D_MODEL), lambda i: (k, i // tiles_per_seq, 0, 0))
    return pl.BlockSpec((1, 1, tm, D_MODEL), lambda i: (k, i, 0, 0))


def _ada_kernel(c_ref, w_ref, b_ref, o_ref):
    s = _silu(c_ref[...]).astype(BF16)
    o_ref[0] = _dot(s, w_ref[0].astype(BF16)) + b_ref[0]


def _ada_call(c_all, w_ada, b_ada):
    n = c_all.shape[0]
    width = N_MOD * D_MODEL
    tn = 1536
    return pl.pallas_call(
        _ada_kernel,
        grid=(DEPTH, width // tn),
        in_specs=[
            pl.BlockSpec((n, D_MODEL), lambda l, j: (0, 0)),
            pl.BlockSpec((1, D_MODEL, tn), lambda l, j: (l, 0, j)),
            pl.BlockSpec((1, 1, tn), lambda l, j: (l, 0, j)),
        ],
        out_specs=pl.BlockSpec((1, n, tn), lambda l, j: (l, 0, j)),
        out_shape=jax.ShapeDtypeStruct((DEPTH, n, width), F32),
        compiler_params=_cparams(("arbitrary", "arbitrary")),
        name="ada_mod",
    )(c_all, w_ada, b_ada.reshape(DEPTH, 1, width))


def _mixin_kernel(has_prev, prompt, n_alias, tiles_per_seq, *refs):
    it = iter(refs)
    xin_ref = next(it)
    if has_prev:
        fin_ref, g2_ref = next(it), next(it)
    lng_ref, lnb_ref, sh_ref, sc_ref = next(it), next(it), next(it), next(it)
    wh_ref, wq_ref, wk_ref, wv_ref, wff_ref, wg_ref = (next(it), next(it), next(it), next(it),
                                                       next(it), next(it))
    bff_ref, bfft_ref = next(it), next(it)
    if prompt:
        tril_ref, triu_ref, plq_ref, plk_ref = next(it), next(it), next(it), next(it)
    for _ in range(n_alias):
        next(it)
    x_ref, hq_ref, hf_ref, hi_ref, hg_ref = next(it), next(it), next(it), next(it), next(it)
    fq_ref, ga_ref, gb_ref = next(it), next(it), next(it)
    k_ref, v_ref, lf_ref = next(it), next(it), next(it)
    if prompt:
        ka_ref, va_ref, carry_c, carry_r = next(it), next(it), next(it), next(it)

    x = xin_ref[...]
    if has_prev:
        x = DEEPNORM_ALPHA * x + g2_ref[0, 0] * fin_ref[...]
    x = _layer_norm(x, lng_ref[...], lnb_ref[...])
    x_ref[...] = x
    h = (x * (1.0 + sc_ref[0, 0]) + sh_ref[0, 0]).astype(BF16)
    h32 = h.astype(F32)

    ph = _dot(h, wh_ref[...], NT_DIMS)
    hq_ref[...] = ph[:, 0 * HG_WIDTH:1 * HG_WIDTH]
    hf_ref[...] = ph[:, 1 * HG_WIDTH:2 * HG_WIDTH]
    hi_ref[...] = ph[:, 2 * HG_WIDTH:3 * HG_WIDTH]
    hg_ref[...] = ph[:, 3 * HG_WIDTH:4 * HG_WIDTH]
    q = _dot(h, wq_ref[...], NT_DIMS) * (FOX_HEAD_DIM ** -0.5)
    pg = _dot(h, wg_ref[...], NT_DIMS)
    ga_ref[...] = pg[:, 0:D_MODEL]
    gb_ref[...] = pg[:, D_MODEL:2 * D_MODEL]

    if not prompt:
        fq_ref[...] = q
        k_ref[...] = _dot(h, wk_ref[...], NT_DIMS)
        v_ref[...] = _dot(h, wv_ref[...], NT_DIMS)
        lf_ref[...] = _log_sigmoid(_dot(h32, wff_ref[...], NT_DIMS) + bff_ref[...])
    else:
        kvf = _dot(wk_ref[...], h, NT_DIMS)
        kt = kvf[0:FOX_WIDTH, :]
        vt = kvf[FOX_WIDTH:2 * FOX_WIDTH, :]
        k_ref[0, 0] = kt
        v_ref[0, 0] = vt
        lf = _log_sigmoid(_dot(h32, wff_ref[...], NT_DIMS) + bff_ref[...])
        lft = _log_sigmoid(kvf[2 * FOX_WIDTH:2 * FOX_WIDTH + FOX_HEADS, :] + bfft_ref[...])
        lf_ref[0, 0] = lft

        @pl.when(pl.program_id(0) % tiles_per_seq == 0)
        def _():
            carry_c[...] = jnp.zeros_like(carry_c)
            carry_r[...] = jnp.zeros_like(carry_r)

        fcol = _dot_sel_lhs(tril_ref[...], lf) + carry_c[...]
        frow = _dot_sel_rhs(lft, triu_ref[...]) + carry_r[...]
        tm = lf.shape[0]
        carry_c[...] = fcol[tm - 1:tm, :]
        carry_r[...] = frow[:, tm - 1:tm]

        low = lax.broadcasted_iota(jnp.int32, (1, LANES), 1) < FOX_HEAD_DIM
        q_aug = _dot(jnp.concatenate(list(_split3(fcol)) + [jnp.ones((tm, FOX_HEADS), F32)], axis=1),
                     plq_ref[...])
        k_aug = _dot(plk_ref[...],
                     jnp.concatenate([-r for r in _split3(frow)] + [jnp.ones((FOX_HEADS, tm), F32)], axis=0))
        ones = jnp.ones((FOX_HEAD_DIM, tm), F32)
        for hd in range(FOX_HEADS):
            feat = slice(hd * FOX_HEAD_DIM, (hd + 1) * FOX_HEAD_DIM)
            blk = slice(hd * LANES, (hd + 1) * LANES)
            src = q[:, (hd // 2) * LANES:(hd // 2 + 1) * LANES]
            if hd % 2 == 0:
                fq_ref[:, blk] = jnp.where(low, src, q_aug[:, blk]).astype(BF16)
                ka_ref[0, blk, :] = jnp.concatenate([kt[feat, :], k_aug[feat, :]], axis=0).astype(BF16)
                va_ref[0, blk, :] = jnp.concatenate([vt[feat, :], ones], axis=0).astype(BF16)
            else:
                fq_ref[:, blk] = jnp.where(low, q_aug[:, blk], src).astype(BF16)
                ka_ref[0, blk, :] = jnp.concatenate([k_aug[feat, :], kt[feat, :]], axis=0).astype(BF16)
                va_ref[0, blk, :] = jnp.concatenate([ones, vt[feat, :]], axis=0).astype(BF16)


def _bias_placements():
    plq = np.zeros((4 * FOX_HEADS, FOX_HEADS * LANES), np.float32)
    plk = np.zeros((FOX_HEADS * FOX_HEAD_DIM, 4 * FOX_HEADS), np.float32)
    for hd in range(FOX_HEADS):
        spare = hd * LANES + (FOX_HEAD_DIM if hd % 2 == 0 else 0)
        for j in range(3):
            plq[j * FOX_HEADS + hd, spare + j] = 1.0
            plq[3 * FOX_HEADS + hd, spare + 3 + j] = 1.0
            plk[hd * FOX_HEAD_DIM + j, 3 * FOX_HEADS + hd] = 1.0
            plk[hd * FOX_HEAD_DIM + 3 + j, j * FOX_HEADS + hd] = 1.0
    return jnp.asarray(plq), jnp.asarray(plk)


def _mixin_call(layer, x_in, f_in, prev_mod, ln_g, ln_b, mod, w, n_seq, seq_len, tm, prompt, kv_bufs):
    T = x_in.shape[0]
    tiles_per_seq = max(seq_len // tm, 1)
    tok = lambda width: pl.BlockSpec((tm, width), lambda i: (i, 0))
    has_prev = f_in is not None

    args, specs = [x_in], [tok(D_MODEL)]
    if has_prev:
        args += [f_in, prev_mod]
        specs += [tok(D_MODEL), _mod_spec(prev_mod, 5, tm, tiles_per_seq)]
    args += [ln_g.reshape(1, D_MODEL), ln_b.reshape(1, D_MODEL), mod, mod]
    specs += [_const_spec((1, D_MODEL)), _const_spec((1, D_MODEL)),
              _mod_spec(mod, 0, tm, tiles_per_seq), _mod_spec(mod, 1, tm, tiles_per_seq)]
    if prompt:
        wnames = ("wh", "wq", "wkvf", "wff", "wff", "wg", "bff", "bfft")
    else:
        wnames = ("wh", "wq", "wk", "wv", "wff", "wg", "bff", "bfft")
    args += [w[n] for n in wnames]
    specs += [_const_spec(w[n].shape) for n in wnames]
    if prompt:
        plq, plk = _bias_placements()
        args += [_tri(tm, True), _tri(tm, False), plq, plk]
        specs += [_const_spec((tm, tm)), _const_spec((tm, tm)), _const_spec(plq.shape), _const_spec(plk.shape)]

    names = ["x", "hq", "hf", "hi", "hg", "fq", "ga", "gb", "k", "v", "lf"]
    widths = [D_MODEL, HG_WIDTH, HG_WIDTH, HG_WIDTH, HG_WIDTH,
              FOX_HEADS * LANES if prompt else FOX_WIDTH, D_MODEL, D_MODEL]
    dtypes = [F32] * 5 + [BF16 if prompt else F32] + [F32] * 2
    out_shape = [jax.ShapeDtypeStruct((T, wd), dt) for wd, dt in zip(widths, dtypes)]
    out_specs = [tok(wd) for wd in widths]
    scratch = []
    aliases = {}
    if prompt:
        fm = lambda rows: pl.BlockSpec((1, 1, rows, tm),
                                       lambda i: (i // tiles_per_seq, layer, 0, i % tiles_per_seq))
        out_shape += [jax.ShapeDtypeStruct((n_seq, DEPTH, FOX_WIDTH, seq_len), F32),
                      jax.ShapeDtypeStruct((n_seq, DEPTH, FOX_WIDTH, seq_len), F32),
                      jax.ShapeDtypeStruct((n_seq, DEPTH, FOX_HEADS, seq_len), F32)]
        out_specs += [fm(FOX_WIDTH), fm(FOX_WIDTH), fm(FOX_HEADS)]
        if kv_bufs is not None:
            for j, buf in enumerate(kv_bufs):
                aliases[len(args)] = 8 + j
                args.append(buf)
                specs.append(pl.BlockSpec(memory_space=pl.ANY))
        names += ["ka", "va"]
        aug = jax.ShapeDtypeStruct((n_seq, FOX_HEADS * LANES, seq_len), BF16)
        aug_spec = pl.BlockSpec((1, FOX_HEADS * LANES, tm),
                                lambda i: (i // tiles_per_seq, 0, i % tiles_per_seq))
        out_shape += [aug, aug]
        out_specs += [aug_spec, aug_spec]
        scratch = [pltpu.VMEM((1, FOX_HEADS), F32), pltpu.VMEM((FOX_HEADS, 1), F32)]
    else:
        out_shape += [jax.ShapeDtypeStruct((T, FOX_WIDTH), F32), jax.ShapeDtypeStruct((T, FOX_WIDTH), F32),
                      jax.ShapeDtypeStruct((T, FOX_HEADS), F32)]
        out_specs += [tok(FOX_WIDTH), tok(FOX_WIDTH), tok(FOX_HEADS)]

    outs = pl.pallas_call(
        functools.partial(_mixin_kernel, has_prev, prompt, len(aliases), tiles_per_seq),
        grid=(T // tm,),
        in_specs=specs,
        out_specs=out_specs,
        out_shape=out_shape,
        scratch_shapes=scratch,
        input_output_aliases=aliases,
        compiler_params=_cparams(("arbitrary",)),
        name="mixer_in",
    )(*args)
    return dict(zip(names, outs))


def _cumsum_rows(x):
    n = x.shape[0]
    row = lax.broadcasted_iota(jnp.int32, x.shape, 0)
    shift = 1
    while shift < n:
        x = x + jnp.where(row >= shift, pltpu.roll(x, shift, 0), 0.0)
        shift *= 2
    return x


def _hgrn_kernel(layer, chunk, sub, n_chunks, hq_ref, hf_ref, hi_ref, hg_ref, lb_ref, ng_ref, s0_ref,
                 o_ref, sout_ref, st_ref):
    t = pl.program_id(1)
    mm = BF16 if chunk >= 16 else F32

    @pl.when(t == 0)
    def _():
        for hd in range(HG_HEADS):
            st_ref[hd] = s0_ref[0, hd].T

    lb_all = lb_ref[...]
    e = jnp.exp(lb_all - jnp.max(lb_all, axis=0, keepdims=True))
    p = e / jnp.sum(e, axis=0, keepdims=True)
    lb = jnp.zeros((1, HG_WIDTH), F32)
    for j in range(1, layer + 1):
        lb = lb + p[j:j + 1, :]
    log_lb = jnp.log(jnp.maximum(lb, LB_FLOOR))
    log1m_lb = jnp.log1p(-lb)
    n_sub = chunk // sub
    causal = (lax.broadcasted_iota(jnp.int32, (chunk, chunk), 1)
              <= lax.broadcasted_iota(jnp.int32, (chunk, chunk), 0))

    def chunk_body(ci, carry):
        r0 = pl.multiple_of(ci * chunk, chunk)
        for hd in range(HG_HEADS):
            ls = slice(hd * HG_DK, (hd + 1) * HG_DK)
            z = hf_ref[pl.ds(r0, chunk), ls]
            q = _silu(hq_ref[pl.ds(r0, chunk), ls])
            v = hi_ref[pl.ds(r0, chunk), ls].astype(mm)
            a = log_lb[:, ls]
            b = log1m_lb[:, ls] + _log_sigmoid(z)
            log_f = jnp.maximum(a, b) + jnp.log1p(jnp.exp(-jnp.abs(a - b)))
            k = (1.0 - lb[:, ls]) * jax.nn.sigmoid(-z)
            cum = _cumsum_rows(log_f)
            blocks = []
            for bi in range(n_sub):
                rows = slice(bi * sub, (bi + 1) * sub)
                base = jnp.zeros((1, HG_DK), F32) if bi == 0 else cum[bi * sub - 1:bi * sub, :]
                k_i = (k * jnp.exp(jnp.minimum(base - cum, EXP_CLAMP))).astype(mm)
                q_i = (q[rows, :] * jnp.exp(cum[rows, :] - base)).astype(mm)
                blocks.append(_dot(q_i, k_i, NT_DIMS))
            attn = blocks[0] if n_sub == 1 else jnp.concatenate(blocks, axis=0)
            attn = jnp.where(causal, attn, 0.0).astype(mm)
            s_t = st_ref[hd]
            o = _dot(attn, v) + _dot((q * jnp.exp(cum)).astype(mm), s_t.astype(mm), NT_DIMS)
            cum_end = cum[chunk - 1:chunk, :]
            k_end = (k * jnp.exp(cum_end - cum)).astype(mm)
            st_ref[hd] = jnp.exp(cum_end) * s_t + _dot(v, k_end, TN_DIMS)
            o = o * lax.rsqrt(jnp.mean(o * o, axis=-1, keepdims=True) + RMS_EPS)
            o_ref[pl.ds(r0, chunk), ls] = o * ng_ref[:, ls] * _silu(hg_ref[pl.ds(r0, chunk), ls])
        return carry

    lax.fori_loop(0, n_chunks, chunk_body, 0, unroll=min(2, n_chunks))

    @pl.when(t == pl.num_programs(1) - 1)
    def _():
        for hd in range(HG_HEADS):
            sout_ref[0, hd] = st_ref[hd].T


def _hgrn_call(layer, p, lower_bounds, norm_g, s0, n_seq, seq_len):
    chunk = HG_CHUNK if seq_len % HG_CHUNK == 0 else seq_len
    sub = min(HG_SUB, chunk)
    tb = min(seq_len, 512)
    nt = seq_len // tb
    tok = pl.BlockSpec((tb, HG_WIDTH), lambda b, t: (b * nt + t, 0))
    st_spec = pl.BlockSpec((1, HG_HEADS, HG_DK, HG_DK), lambda b, t: (b, 0, 0, 0))
    return pl.pallas_call(
        functools.partial(_hgrn_kernel, layer, chunk, sub, tb // chunk),
        grid=(n_seq, nt),
        in_specs=[tok, tok, tok, tok,
                  pl.BlockSpec((DEPTH, HG_WIDTH), lambda b, t: (0, 0)),
                  pl.BlockSpec((1, HG_WIDTH), lambda b, t: (0, 0)),
                  st_spec],
        out_specs=[tok, st_spec],
        out_shape=[jax.ShapeDtypeStruct((n_seq * seq_len, HG_WIDTH), F32),
                   jax.ShapeDtypeStruct((n_seq, HG_HEADS, HG_DK, HG_DK), F32)],
        scratch_shapes=[pltpu.VMEM((HG_HEADS, HG_DK, HG_DK), F32)],
        compiler_params=_cparams(("arbitrary", "arbitrary")),
        name="hgrn",
    )(p["hq"], p["hf"], p["hi"], p["hg"], lower_bounds, norm_g.reshape(1, HG_WIDTH), s0)


def _fox_kernel(tq, tk, q_ref, k_ref, v_ref, o_ref, m_ref, acc_ref):
    qi = pl.program_id(1)
    ki = pl.program_id(2)

    @pl.when(ki == 0)
    def _():
        m_ref[...] = jnp.full_like(m_ref, MASK_VALUE)
        acc_ref[...] = jnp.zeros_like(acc_ref)

    def absorb(diagonal):
        if diagonal:
            visible = (lax.broadcasted_iota(jnp.int32, (tq, tk), 1)
                       <= lax.broadcasted_iota(jnp.int32, (tq, tk), 0))
        for hd in range(FOX_HEADS):
            blk = slice(hd * LANES, (hd + 1) * LANES)
            s = _dot(q_ref[:, blk], k_ref[0, blk, :])
            if diagonal:
                s = jnp.where(visible, s, MASK_VALUE)
            cols = [s[:, c * LANES:(c + 1) * LANES] for c in range(tk // LANES)]
            cmax = cols[0]
            for c in cols[1:]:
                cmax = jnp.maximum(cmax, c)
            m_old = m_ref[hd]
            m_new = jnp.maximum(m_old, jnp.broadcast_to(jnp.max(cmax, axis=-1, keepdims=True), (tq, LANES)))
            pe = jnp.concatenate([jnp.exp(c - m_new) for c in cols], axis=1).astype(BF16)
            m_ref[hd] = m_new
            acc_ref[:, blk] = jnp.exp(m_old - m_new) * acc_ref[:, blk] + _dot(pe, v_ref[0, blk, :], NT_DIMS)

    @pl.when(ki < qi)
    def _():
        absorb(False)

    @pl.when(ki == qi)
    def _():
        absorb(True)
        low = lax.broadcasted_iota(jnp.int32, (1, LANES), 1) < FOX_HEAD_DIM
        for pr in range(FOX_HEADS // 2):
            even = acc_ref[:, 2 * pr * LANES:(2 * pr + 1) * LANES]
            odd = acc_ref[:, (2 * pr + 1) * LANES:(2 * pr + 2) * LANES]
            o_ref[:, pr * LANES:(pr + 1) * LANES] = jnp.where(
                low, even / pltpu.roll(even, FOX_HEAD_DIM, 1), odd / pltpu.roll(odd, FOX_HEAD_DIM, 1))


def _fox_call(p, n_seq, seq_len, tq):
    tk = tq
    nq = seq_len // tq
    T = n_seq * seq_len
    width = FOX_HEADS * LANES
    kspec = pl.BlockSpec((1, width, tk), lambda b, qi, ki: (b, 0, jnp.minimum(ki, qi)))
    return pl.pallas_call(
        functools.partial(_fox_kernel, tq, tk),
        grid=(n_seq, nq, nq),
        in_specs=[pl.BlockSpec((tq, width), lambda b, qi, ki: (b * nq + qi, 0)), kspec, kspec],
        out_specs=pl.BlockSpec((tq, FOX_WIDTH), lambda b, qi, ki: (b * nq + qi, 0)),
        out_shape=jax.ShapeDtypeStruct((T, FOX_WIDTH), F32),
        scratch_shapes=[pltpu.VMEM((FOX_HEADS, tq, LANES), F32), pltpu.VMEM((tq, width), F32)],
        compiler_params=_cparams(("arbitrary", "arbitrary", "arbitrary")),
        name="fox_prompt",
    )(p["fq"], p["ka"], p["va"])


PAGES_PER_STEP = 16


def _decode_kernel(n_new, pt_ref, q_ref, kn_ref, vn_ref, lfn_ref, *refs):
    npg = PAGES_PER_STEP
    k_refs = refs[0:npg]
    v_refs = refs[npg:2 * npg]
    lf_refs = refs[2 * npg:3 * npg]
    o_ref, m_ref, l_ref, acc_ref, carry_ref = refs[3 * npg:]
    j = pl.program_id(1)
    rows = FOX_HEADS * n_new

    @pl.when(j == 0)
    def _():
        m_ref[...] = jnp.full_like(m_ref, MASK_VALUE)
        l_ref[...] = jnp.zeros_like(l_ref)
        acc_ref[...] = jnp.zeros_like(acc_ref)
        carry_ref[...] = jnp.zeros_like(carry_ref)

    q = q_ref[...]
    rr = lax.broadcasted_iota(jnp.int32, (rows, FOX_WIDTH), 0)
    cc = lax.broadcasted_iota(jnp.int32, (rows, FOX_WIDTH), 1)
    q_rep = jnp.concatenate([q] * FOX_HEADS, axis=0)
    qbd = jnp.where((rr // n_new) == (cc // FOX_HEAD_DIM), q_rep, 0.0)
    qbd_bf = qbd.astype(BF16)
    ehe = jnp.where(lax.broadcasted_iota(jnp.int32, (rows, FOX_HEADS), 0) // n_new
                    == lax.broadcasted_iota(jnp.int32, (rows, FOX_HEADS), 1), 1.0, 0.0)
    triu = _tri(PAGE_SIZE, False)

    def absorb(s, pv_fn):
        m_old = m_ref[...]
        m_new = jnp.maximum(m_old, jnp.max(s, axis=-1, keepdims=True))
        alpha = jnp.exp(m_old - m_new)
        pe = jnp.exp(s - m_new)
        l_ref[...] = alpha * l_ref[...] + jnp.sum(pe, axis=-1, keepdims=True)
        m_ref[...] = m_new
        acc_ref[...] = alpha * acc_ref[...] + pv_fn(pe)

    lf_all = jnp.concatenate([lf_refs[pg][...] for pg in range(npg)], axis=0)
    within = _dot_sel_rhs(lf_all, triu)
    nr = npg * FOX_HEADS
    ri = lax.broadcasted_iota(jnp.int32, (nr, nr), 0)
    ci = lax.broadcasted_iota(jnp.int32, (nr, nr), 1)
    earlier = jnp.where(jnp.logical_and(ri % FOX_HEADS == ci % FOX_HEADS, ci < ri), 1.0, 0.0)
    totals = jnp.broadcast_to(within[:, PAGE_SIZE - 1:PAGE_SIZE], (nr, PAGE_SIZE))
    carry_rep = jnp.concatenate([jnp.broadcast_to(carry_ref[...], (FOX_HEADS, PAGE_SIZE))] * npg, axis=0)
    f_all = within + _dot_sel_lhs(earlier, totals) + carry_rep
    carry_ref[...] = f_all[nr - FOX_HEADS:nr, PAGE_SIZE - 1:PAGE_SIZE]
    bias = jnp.concatenate(
        [jnp.concatenate([jnp.broadcast_to(f_all[pg * FOX_HEADS + hd:pg * FOX_HEADS + hd + 1, :],
                                           (n_new, PAGE_SIZE)) for hd in range(FOX_HEADS)], axis=0)
         for pg in range(npg)], axis=1)
    kt = jnp.concatenate([k_refs[pg][...].astype(BF16) for pg in range(npg)], axis=1)
    vt = jnp.concatenate([v_refs[pg][...].astype(BF16) for pg in range(npg)], axis=1)
    absorb(_dot(qbd_bf, kt) - bias, lambda pe: _dot(pe.astype(BF16), vt, NT_DIMS))

    @pl.when(j == pl.num_programs(1) - 1)
    def _():
        kn = kn_ref[...]
        vn = vn_ref[...]
        eye = (lax.broadcasted_iota(jnp.int32, (FOX_HEADS, FOX_HEADS), 0)
               == lax.broadcasted_iota(jnp.int32, (FOX_HEADS, FOX_HEADS), 1))
        carry_row = jnp.sum(jnp.where(eye, carry_ref[...], 0.0), axis=0, keepdims=True)
        f_new = _dot_sel_lhs(_tri(n_new, True), lfn_ref[...]) + carry_row
        s = _dot(qbd, kn, NT_DIMS) - _dot_sel_lhs(ehe, f_new, NT_DIMS)
        key_i = lax.broadcasted_iota(jnp.int32, (rows, n_new), 1)
        qry_i = lax.broadcasted_iota(jnp.int32, (rows, n_new), 0) % n_new
        s = jnp.where(key_i <= qry_i, s, MASK_VALUE)
        absorb(s, lambda pe: _dot(pe, vn))
        out = acc_ref[...] / l_ref[...]
        lane_head = lax.broadcasted_iota(jnp.int32, (n_new, FOX_WIDTH), 1) // FOX_HEAD_DIM
        res = jnp.zeros((n_new, FOX_WIDTH), F32)
        for hd in range(FOX_HEADS):
            res = res + jnp.where(lane_head == hd, out[hd * n_new:(hd + 1) * n_new, :], 0.0)
        o_ref[...] = res


def _decode_call(layer, p, cache_kt, cache_vt, cache_lft, page_table, n_seq, n_new):
    n_pages = page_table.shape[1]
    npg = PAGES_PER_STEP
    pt_flat = page_table.reshape(-1).astype(jnp.int32)

    def page_map(pg):
        return lambda n, j, pt: (pt[n * n_pages + j * npg + pg], layer, 0, 0)

    new_w = pl.BlockSpec((n_new, FOX_WIDTH), lambda n, j, pt: (n, 0))
    in_specs = [new_w, new_w, new_w, pl.BlockSpec((n_new, FOX_HEADS), lambda n, j, pt: (n, 0))]
    in_specs += [pl.BlockSpec((None, None, FOX_WIDTH, PAGE_SIZE), page_map(pg)) for pg in range(npg)]
    in_specs += [pl.BlockSpec((None, None, FOX_WIDTH, PAGE_SIZE), page_map(pg)) for pg in range(npg)]
    in_specs += [pl.BlockSpec((None, None, FOX_HEADS, PAGE_SIZE), page_map(pg)) for pg in range(npg)]
    rows = FOX_HEADS * n_new
    grid_spec = pltpu.PrefetchScalarGridSpec(
        num_scalar_prefetch=1,
        grid=(n_seq, n_pages // npg),
        in_specs=in_specs,
        out_specs=new_w,
        scratch_shapes=[pltpu.VMEM((rows, 1), F32), pltpu.VMEM((rows, 1), F32),
                        pltpu.VMEM((rows, FOX_WIDTH), F32), pltpu.VMEM((FOX_HEADS, 1), F32)],
    )
    return pl.pallas_call(
        functools.partial(_decode_kernel, n_new),
        grid_spec=grid_spec,
        out_shape=jax.ShapeDtypeStruct((n_seq * n_new, FOX_WIDTH), F32),
        compiler_params=_cparams(("arbitrary", "arbitrary")),
        name="fox_decode",
    )(pt_flat, p["fq"], p["k"], p["v"], p["lf"],
      *([cache_kt] * npg), *([cache_vt] * npg), *([cache_lft] * npg))


def _merge_kernel(tiles_per_win, part, oa_ref, ob_ref, ga_ref, gb_ref, x_ref, g1_ref, sh_ref, sc_ref,
                  wa_ref, wb_ref, wo_ref, lng_ref, lnb_ref, wrt_ref, br_ref, triu_ref,
                  x1_ref, h2_ref, cw_ref, route_ref, carry_ref):
    @pl.when(pl.program_id(0) % tiles_per_win == 0)
    def _():
        carry_ref[...] = jnp.zeros_like(carry_ref)

    for pi in range(x_ref.shape[0] // part):
        _merge_part(slice(pi * part, (pi + 1) * part), oa_ref, ob_ref, ga_ref, gb_ref, x_ref, g1_ref,
                    sh_ref, sc_ref, wa_ref, wb_ref, wo_ref, lng_ref, lnb_ref, wrt_ref, br_ref, triu_ref,
                    x1_ref, h2_ref, cw_ref, route_ref, carry_ref)


def _merge_part(rs, oa_ref, ob_ref, ga_ref, gb_ref, x_ref, g1_ref, sh_ref, sc_ref,
                wa_ref, wb_ref, wo_ref, lng_ref, lnb_ref, wrt_ref, br_ref, triu_ref,
                x1_ref, h2_ref, cw_ref, route_ref, carry_ref):
    def mod_rows(ref):
        return ref[0, 0] if ref.shape[2] == 1 else ref[0, 0, rs, :]

    ya = _dot(oa_ref[rs, :].astype(BF16), wa_ref[...])
    yb = _dot(ob_ref[rs, :].astype(BF16), wb_ref[...])
    merged = jax.nn.sigmoid(ga_ref[rs, :]) * ya + jax.nn.sigmoid(gb_ref[rs, :]) * yb
    m = _dot(merged.astype(BF16), wo_ref[...])
    x1 = _layer_norm(DEEPNORM_ALPHA * x_ref[rs, :] + mod_rows(g1_ref) * m, lng_ref[...], lnb_ref[...])
    x1_ref[rs, :] = x1
    h2 = x1 * (1.0 + mod_rows(sc_ref)) + mod_rows(sh_ref)
    h2_ref[rs, :] = h2.astype(BF16)

    tm = h2.shape[0]
    hp = [t.astype(BF16) for t in _split3(h2)]
    wr = [wrt_ref[j] for j in range(3)]
    small = _dot(hp[1], wr[1]) + _dot(hp[0], wr[2]) + _dot(hp[2], wr[0])
    mid = _dot(hp[0], wr[1]) + _dot(hp[1], wr[0])
    logits = (small + mid + _dot(hp[0], wr[0])).T[:N_EXPERTS, :] + br_ref[...]
    ex = jnp.exp(logits - jnp.max(logits, axis=0, keepdims=True))
    scores = ex / jnp.sum(ex, axis=0, keepdims=True)
    gs = []
    for g in range(N_GROUPS):
        r = [scores[g * EXPERTS_PER_GROUP + e:g * EXPERTS_PER_GROUP + e + 1, :]
             for e in range(EXPERTS_PER_GROUP)]
        best = r[0] + r[1]
        for a in range(EXPERTS_PER_GROUP):
            for b in range(a + 1, EXPERTS_PER_GROUP):
                best = jnp.maximum(best, r[a] + r[b])
        gs.append(best)
    gmax = jnp.maximum(jnp.maximum(gs[0], gs[1]), jnp.maximum(gs[2], gs[3]))
    gid = jnp.where(gs[0] == gmax, 0, jnp.where(gs[1] == gmax, 1, jnp.where(gs[2] == gmax, 2, 3)))
    erow = lax.broadcasted_iota(jnp.int32, (N_EXPERTS, tm), 0)
    masked = jnp.where(erow // EXPERTS_PER_GROUP == gid, scores, -1.0)
    top1 = jnp.max(masked, axis=0, keepdims=True)
    idx1 = jnp.min(jnp.where(masked == top1, erow, N_EXPERTS), axis=0, keepdims=True)
    masked2 = jnp.where(erow == idx1, -2.0, masked)
    top2 = jnp.max(masked2, axis=0, keepdims=True)
    idx2 = jnp.min(jnp.where(masked2 == top2, erow, N_EXPERTS), axis=0, keepdims=True)
    den = top1 + top2
    cw_ref[:, rs] = jnp.where(erow == idx1, top1 / den, 0.0) + jnp.where(erow == idx2, top2 / den, 0.0)

    grow = lax.broadcasted_iota(jnp.int32, (8, tm), 0)
    member = grow == gid
    incl = _dot(jnp.where(member, 1.0, 0.0), triu_ref[...]) + carry_ref[...]
    carry_ref[...] = incl[:, tm - 1:tm]
    rank = jnp.sum(jnp.where(member, incl, 0.0), axis=0, keepdims=True) - 1.0
    route_ref[:, rs] = jnp.where(grow == 0, gid, jnp.where(grow == 1, rank.astype(jnp.int32), 0))


def _merge_call(o_a, o_b, p, mod, w, ln_g, ln_b, w_router_t, b_router, seq_len, tm, win):
    T = o_a.shape[0]
    part = min(tm, 256)
    tiles_per_seq = max(seq_len // tm, 1)
    tok = lambda width: pl.BlockSpec((tm, width), lambda i: (i, 0))
    return pl.pallas_call(
        functools.partial(_merge_kernel, win // tm, part),
        grid=(T // tm,),
        in_specs=[tok(HG_WIDTH), tok(FOX_WIDTH), tok(D_MODEL), tok(D_MODEL), tok(D_MODEL),
                  _mod_spec(mod, 2, tm, tiles_per_seq), _mod_spec(mod, 3, tm, tiles_per_seq),
                  _mod_spec(mod, 4, tm, tiles_per_seq),
                  _const_spec((HG_WIDTH, D_MODEL)), _const_spec((FOX_WIDTH, D_MODEL)),
                  _const_spec((D_MODEL, D_MODEL)), _const_spec((1, D_MODEL)), _const_spec((1, D_MODEL)),
                  _const_spec((3, D_MODEL, LANES)), _const_spec((N_EXPERTS, 1)), _const_spec((part, part))],
        out_specs=[tok(D_MODEL), tok(D_MODEL),
                   pl.BlockSpec((N_EXPERTS, tm), lambda i: (0, i)),
                   pl.BlockSpec((8, tm), lambda i: (0, i))],
        out_shape=[jax.ShapeDtypeStruct((T, D_MODEL), F32), jax.ShapeDtypeStruct((T, D_MODEL), BF16),
                   jax.ShapeDtypeStruct((N_EXPERTS, T), F32), jax.ShapeDtypeStruct((8, T), jnp.int32)],
        scratch_shapes=[pltpu.VMEM((8, 1), F32)],
        compiler_params=_cparams(("arbitrary",)),
        name="merge",
    )(o_a, o_b, p["ga"], p["gb"], p["x"], mod, mod, mod, w["wa"], w["wb"], w["wo"],
      ln_g.reshape(1, D_MODEL), ln_b.reshape(1, D_MODEL), w_router_t, b_router.reshape(N_EXPERTS, 1),
      _tri(part, False))


MOE_ROWS = 320


def _moe_kernel(h_ref, cw_ref, route_ref, wg_ref, wu_ref, wd_ref, y_ref):
    g = pl.program_id(1)
    win = h_ref.shape[0]

    @pl.when(g == 0)
    def _():
        y_ref[...] = jnp.zeros_like(y_ref)

    gid = route_ref[0:1, :]
    rank = route_ref[1:2, :]
    member = gid == g
    count = jnp.sum(jnp.where(member, 1, 0))
    n_steps = (count + MOE_ROWS - 1) // MOE_ROWS
    slot = lax.broadcasted_iota(jnp.int32, (MOE_ROWS, win), 0)
    ecol = lax.broadcasted_iota(jnp.int32, (MOE_ROWS, N_EXPERTS), 1)

    def body(j, carry):
        hit = jnp.logical_and(member, rank - j * MOE_ROWS == slot)
        perm_f = jnp.where(hit, 1.0, 0.0)
        perm = perm_f.astype(BF16)
        hs = _dot(perm, h_ref[...]).astype(BF16)
        cwg = _dot_sel_lhs(perm_f, cw_ref[...], NT_DIMS)
        acc = jnp.zeros((MOE_ROWS, D_MODEL), F32)
        for e in range(EXPERTS_PER_GROUP):
            cwe = jnp.sum(jnp.where(ecol == g * EXPERTS_PER_GROUP + e, cwg, 0.0), axis=1, keepdims=True)
            a = _dot(hs, wg_ref[0, e])
            u = _dot(hs, wu_ref[0, e])
            hid = (_silu(a) * u * cwe).astype(BF16)
            acc = acc + _dot(hid, wd_ref[0, e])
        y_ref[...] += _dot(perm, acc.astype(BF16), TN_DIMS)
        return carry

    lax.fori_loop(0, n_steps, body, 0)


def _moe_call(h2, cw, route, w, win):
    T = h2.shape[0]
    wspec = lambda shp: pl.BlockSpec((1,) + shp, lambda i, g: (g, 0, 0, 0))
    return pl.pallas_call(
        _moe_kernel,
        grid=(T // win, N_GROUPS),
        in_specs=[pl.BlockSpec((win, D_MODEL), lambda i, g: (i, 0)),
                  pl.BlockSpec((N_EXPERTS, win), lambda i, g: (0, i)),
                  pl.BlockSpec((8, win), lambda i, g: (0, i)),
                  wspec((EXPERTS_PER_GROUP, D_MODEL, D_EXPERT)),
                  wspec((EXPERTS_PER_GROUP, D_MODEL, D_EXPERT)),
                  wspec((EXPERTS_PER_GROUP, D_EXPERT, D_MODEL))],
        out_specs=pl.BlockSpec((win, D_MODEL), lambda i, g: (i, 0)),
        out_shape=jax.ShapeDtypeStruct((T, D_MODEL), F32),
        compiler_params=_cparams(("arbitrary", "arbitrary")),
        name="moe",
    )(h2, cw, route, w["eg"], w["eu"], w["ed"])


def _final_kernel(x_ref, f_ref, g2_ref, lng_ref, lnb_ref, o_ref):
    o_ref[...] = _layer_norm(DEEPNORM_ALPHA * x_ref[...] + g2_ref[0, 0] * f_ref[...],
                             lng_ref[...], lnb_ref[...])


def _final_call(x1, f, mod, ln_g, ln_b, seq_len, tm):
    T = x1.shape[0]
    tok = pl.BlockSpec((tm, D_MODEL), lambda i: (i, 0))
    return pl.pallas_call(
        _final_kernel,
        grid=(T // tm,),
        in_specs=[tok, tok, _mod_spec(mod, 5, tm, max(seq_len // tm, 1)),
                  _const_spec((1, D_MODEL)), _const_spec((1, D_MODEL))],
        out_specs=tok,
        out_shape=jax.ShapeDtypeStruct((T, D_MODEL), F32),
        compiler_params=_cparams(("arbitrary",)),
        name="final_norm",
    )(x1, f, mod, ln_g.reshape(1, D_MODEL), ln_b.reshape(1, D_MODEL))


def _layer_weights(prm, w_in_t, l):
    c0 = 4 * HG_WIDTH
    c1 = c0 + FOX_WIDTH
    c2 = c1 + FOX_WIDTH
    c3 = c2 + FOX_WIDTH
    c4 = c3 + FOX_HEADS
    wt = w_in_t[l]
    grouped = lambda a, shp: a.astype(BF16).reshape((N_GROUPS, EXPERTS_PER_GROUP) + shp)
    return {
        "wh": wt[:c0].astype(BF16),
        "wq": wt[c0:c1].astype(BF16),
        "wk": wt[c1:c2].astype(BF16),
        "wv": wt[c2:c3].astype(BF16),
        "wff": wt[c3:c4].astype(BF16).astype(F32),
        "wkvf": jnp.concatenate([wt[c1:c4].astype(BF16), jnp.zeros((8, D_MODEL), BF16)], axis=0),
        "wg": wt[c4:].astype(BF16),
        "bff": prm["b_fox_f"][l].reshape(1, FOX_HEADS),
        "bfft": prm["b_fox_f"][l].reshape(FOX_HEADS, 1),
        "wa": prm["w_branch_a"][l].astype(BF16),
        "wb": prm["w_branch_b"][l].astype(BF16),
        "wo": prm["w_out"][l].astype(BF16),
        "eg": grouped(prm["w_exp_gate"][l], (D_MODEL, D_EXPERT)),
        "eu": grouped(prm["w_exp_up"][l], (D_MODEL, D_EXPERT)),
        "ed": grouped(prm["w_exp_down"][l], (D_EXPERT, D_MODEL)),
    }


def _trunk(x, mods, hg_state, paged, prm, weights):
    n_seq, seq_len, _ = x.shape
    T = n_seq * seq_len
    prompt = paged is None
    tm = min(256, T)
    win = min(1024, T)
    wr = jnp.pad(prm["w_router"], ((0, 0), (0, LANES - N_EXPERTS)))
    wr1 = wr.astype(BF16)
    wr2 = (wr - wr1.astype(F32)).astype(BF16)
    wr3 = (wr - wr1.astype(F32) - wr2.astype(F32)).astype(BF16)
    w_router_t = jnp.stack([wr1, wr2, wr3])
    if prompt:
        s0_all = jnp.zeros((DEPTH, n_seq, HG_HEADS, HG_DK, HG_DK), F32)
    else:
        s0_all = hg_state.astype(F32)
        cache_k, cache_v, cache_lf, page_table = paged
        n_phys = cache_k.shape[0]
        cache_kt = jnp.transpose(cache_k, (0, 2, 3, 4, 1)).reshape(n_phys, DEPTH, FOX_WIDTH, PAGE_SIZE)
        cache_vt = jnp.transpose(cache_v, (0, 2, 3, 4, 1)).reshape(n_phys, DEPTH, FOX_WIDTH, PAGE_SIZE)
        cache_lft = jnp.transpose(cache_lf, (0, 2, 3, 1))

    ks, vs, lfs, states = [], [], [], []
    kv_bufs = None
    x_in, f_in = x.reshape(T, D_MODEL), None
    for l in range(DEPTH):
        w = weights[l]
        if l == 0:
            ln_g, ln_b, prev_mod = prm["ln_in_g"], prm["ln_in_b"], None
        else:
            ln_g, ln_b, prev_mod = prm["ln2_g"][l - 1], prm["ln2_b"][l - 1], mods[l - 1]
        p = _mixin_call(l, x_in, f_in, prev_mod, ln_g, ln_b, mods[l], w, n_seq, seq_len, tm, prompt, kv_bufs)
        o_a, s_new = _hgrn_call(l, p, prm["hgrn_lower_bounds"], prm["hgrn_norm_g"][l], s0_all[l],
                                n_seq, seq_len)
        if prompt:
            kv_bufs = (p["k"], p["v"], p["lf"])
            o_b = _fox_call(p, n_seq, seq_len, min(512, seq_len))
        else:
            o_b = _decode_call(l, p, cache_kt, cache_vt, cache_lft, page_table, n_seq, seq_len)
            ks.append(p["k"])
            vs.append(p["v"])
            lfs.append(p["lf"])
        x1, h2, cw, route = _merge_call(o_a, o_b, p, mods[l], w, prm["ln1_g"][l], prm["ln1_b"][l],
                                        w_router_t, prm["b_router"], seq_len, min(512, T), win)
        f = _moe_call(h2, cw, route, w, win)
        x_in, f_in = x1, f
        states.append(s_new)
    y = _final_call(x_in, f_in, mods[DEPTH - 1], prm["ln2_g"][DEPTH - 1], prm["ln2_b"][DEPTH - 1], seq_len, tm)
    y = y.reshape(n_seq, seq_len, D_MODEL)
    if prompt:
        kb, vb, lfb = kv_bufs
        k_out = jnp.transpose(kb.reshape(n_seq, DEPTH, FOX_HEADS, FOX_HEAD_DIM, seq_len), (0, 4, 1, 2, 3))
        v_out = jnp.transpose(vb.reshape(n_seq, DEPTH, FOX_HEADS, FOX_HEAD_DIM, seq_len), (0, 4, 1, 2, 3))
        lf_out = jnp.transpose(lfb, (0, 3, 1, 2))
    else:
        k_out = jnp.stack(ks, axis=1).reshape(n_seq, seq_len, DEPTH, FOX_HEADS, FOX_HEAD_DIM)
        v_out = jnp.stack(vs, axis=1).reshape(n_seq, seq_len, DEPTH, FOX_HEADS, FOX_HEAD_DIM)
        lf_out = jnp.stack(lfs, axis=1).reshape(n_seq, seq_len, DEPTH, FOX_HEADS)
    return y, k_out, v_out, lf_out, jnp.stack(states, axis=0)


def kernel(x_prompt, x_sample, c_prompt, c_sample, cache_k, cache_v, cache_logf, state_hgrn, page_table,
           ln_in_g, ln_in_b, w_ada, b_ada, w_in, b_fox_f, hgrn_lower_bounds, hgrn_norm_g,
           w_branch_a, w_branch_b, w_out, ln1_g, ln1_b, w_router, b_router,
           w_exp_gate, w_exp_up, w_exp_down, ln2_g, ln2_b):
    prm = dict(ln_in_g=ln_in_g, ln_in_b=ln_in_b, b_fox_f=b_fox_f,
               hgrn_lower_bounds=hgrn_lower_bounds, hgrn_norm_g=hgrn_norm_g, w_branch_a=w_branch_a,
               w_branch_b=w_branch_b, w_out=w_out, ln1_g=ln1_g, ln1_b=ln1_b, w_router=w_router,
               b_router=b_router, w_exp_gate=w_exp_gate, w_exp_up=w_exp_up, w_exp_down=w_exp_down,
               ln2_g=ln2_g, ln2_b=ln2_b)
    n_p, n_s = x_prompt.shape[0], x_sample.shape[0]
    dec_seq = x_sample.shape[1]
    mod_all = _ada_call(jnp.concatenate([c_prompt, c_sample], axis=0), w_ada, b_ada)
    mods_p, mods_s = [], []
    for l in range(DEPTH):
        mp = mod_all[l, :n_p].reshape(n_p, N_MOD, D_MODEL).transpose(1, 0, 2)
        mods_p.append(mp[:, :, None, :])
        ms = mod_all[l, n_p:].reshape(n_s, N_MOD, D_MODEL).transpose(1, 0, 2)
        mods_s.append(jnp.repeat(ms, dec_seq, axis=1)[:, None, :, :])
    w_in_t = jnp.transpose(w_in, (0, 2, 1))
    weights = [_layer_weights(prm, w_in_t, l) for l in range(DEPTH)]

    y_p, k_p, v_p, lf_p, hg_p = _trunk(x_prompt, mods_p, None, None, prm, weights)
    y_s, k_s, v_s, lf_s, hg_s = _trunk(x_sample, mods_s, state_hgrn,
                                       (cache_k, cache_v, cache_logf, page_table), prm, weights)
    return (y_p, y_s, k_p, v_p, lf_p, hg_p.astype(x_prompt.dtype),
            k_s, v_s, lf_s, hg_s.astype(state_hgrn.dtype))
```

```python
import functools

import jax
import jax.numpy as jnp
import numpy as np
from jax import lax
from jax.experimental import pallas as pl
from jax.experimental.pallas import tpu as pltpu

F32 = jnp.float32
BF16 = jnp.bfloat16

D_MODEL = 1024
DEPTH = 4
PAGE_SIZE = 128
HG_WIDTH = 512
HG_HEADS = 4
HG_DK = 128
HG_CHUNK = 64
HG_SUB = 16
LB_FLOOR = 1e-30
EXP_CLAMP = 80.0
FOX_HEADS = 8
FOX_HEAD_DIM = 64
FOX_WIDTH = 512
MASK_VALUE = -1e30
N_EXPERTS = 16
N_GROUPS = 4
EXPERTS_PER_GROUP = 4
D_EXPERT = 512
N_MOD = 6
DEEPNORM_ALPHA = (2 * DEPTH) ** 0.25
LN_EPS = 1e-5
RMS_EPS = 1e-6
LANES = 128
VMEM_LIMIT = 56 * 1024 * 1024

NN_DIMS = (((1,), (0,)), ((), ()))
NT_DIMS = (((1,), (1,)), ((), ()))
TN_DIMS = (((0,), (0,)), ((), ()))


def _cparams(sem):
    return pltpu.CompilerParams(dimension_semantics=sem, vmem_limit_bytes=VMEM_LIMIT)


def _dot(a, b, dims=NN_DIMS):
    return lax.dot_general(a, b, dims, preferred_element_type=F32)


def _split3(x):
    x1 = x.astype(BF16).astype(F32)
    r1 = x - x1
    x2 = r1.astype(BF16).astype(F32)
    x3 = (r1 - x2).astype(BF16).astype(F32)
    return (x1, x2, x3)


def _dot_sel_lhs(sel, b, dims=NN_DIMS):
    b1, b2, b3 = _split3(b)
    return _dot(sel, b3, dims) + _dot(sel, b2, dims) + _dot(sel, b1, dims)


def _dot_sel_rhs(a, sel, dims=NN_DIMS):
    a1, a2, a3 = _split3(a)
    return _dot(a3, sel, dims) + _dot(a2, sel, dims) + _dot(a1, sel, dims)


def _dot_f32(a, b, dims=NN_DIMS):
    a1, a2, a3 = _split3(a)
    b1, b2, b3 = _split3(b)
    small = _dot(a2, b2, dims) + _dot(a1, b3, dims) + _dot(a3, b1, dims)
    mid = _dot(a1, b2, dims) + _dot(a2, b1, dims)
    return small + mid + _dot(a1, b1, dims)


def _layer_norm(x, g, b):
    xc = x - jnp.mean(x, axis=-1, keepdims=True)
    var = jnp.mean(xc * xc, axis=-1, keepdims=True)
    return xc * lax.rsqrt(var + LN_EPS) * g + b


def _silu(x):
    return x * jax.nn.sigmoid(x)


def _log_sigmoid(x):
    return jnp.minimum(x, 0.0) - jnp.log1p(jnp.exp(-jnp.abs(x)))


def _const_spec(shape):
    nd = len(shape)
    return pl.BlockSpec(shape, lambda *_: (0,) * nd)


def _tri(n, lower):
    r = lax.broadcasted_iota(jnp.int32, (n, n), 0)
    c = lax.broadcasted_iota(jnp.int32, (n, n), 1)
    return jnp.where((r >= c) if lower else (r <= c), 1.0, 0.0).astype(F32)


def _mod_spec(mod, k, tm, tiles_per_seq):
    if mod.shape[2] == 1:
        return pl.BlockSpec((1, 1, 1, D_MODEL), lambda i: (k, i // tiles_per_seq, 0, 0))
    return pl.BlockSpec((1, 1, tm, D_MODEL), lambda i: (k, i, 0, 0))


def _ada_kernel(c_ref, w_ref, b_ref, o_ref):
    s = _silu(c_ref[...]).astype(BF16)
    o_ref[0] = _dot(s, w_ref[0].astype(BF16)) + b_ref[0]


def _ada_call(c_all, w_ada, b_ada):
    n = c_all.shape[0]
    width = N_MOD * D_MODEL
    tn = 1536
    return pl.pallas_call(
        _ada_kernel,
        grid=(DEPTH, width // tn),
        in_specs=[
            pl.BlockSpec((n, D_MODEL), lambda l, j: (0, 0)),
            pl.BlockSpec((1, D_MODEL, tn), lambda l, j: (l, 0, j)),
            pl.BlockSpec((1, 1, tn), lambda l, j: (l, 0, j)),
        ],
        out_specs=pl.BlockSpec((1, n, tn), lambda l, j: (l, 0, j)),
        out_shape=jax.ShapeDtypeStruct((DEPTH, n, width), F32),
        compiler_params=_cparams(("arbitrary", "arbitrary")),
        name="ada_mod",
    )(c_all, w_ada, b_ada.reshape(DEPTH, 1, width))


def _mixin_kernel(has_prev, prompt, n_alias, tiles_per_seq, *refs):
    it = iter(refs)
    xin_ref = next(it)
    if has_prev:
        fin_ref, g2_ref = next(it), next(it)
    lng_ref, lnb_ref, sh_ref, sc_ref = next(it), next(it), next(it), next(it)
    wh_ref, wq_ref, wk_ref, wv_ref, wff_ref, wg_ref = (next(it), next(it), next(it), next(it),
                                                       next(it), next(it))
    bff_ref, bfft_ref = next(it), next(it)
    if prompt:
        tril_ref, triu_ref, plq_ref, plk_ref = next(it), next(it), next(it), next(it)
    for _ in range(n_alias):
        next(it)
    x_ref, hq_ref, hf_ref, hi_ref, hg_ref = next(it), next(it), next(it), next(it), next(it)
    fq_ref, ga_ref, gb_ref = next(it), next(it), next(it)
    k_ref, v_ref, lf_ref = next(it), next(it), next(it)
    if prompt:
        ka_ref, va_ref, carry_c, carry_r = next(it), next(it), next(it), next(it)

    x = xin_ref[...]
    if has_prev:
        x = DEEPNORM_ALPHA * x + g2_ref[0, 0] * fin_ref[...]
    x = _layer_norm(x, lng_ref[...], lnb_ref[...])
    x_ref[...] = x
    h = (x * (1.0 + sc_ref[0, 0]) + sh_ref[0, 0]).astype(BF16)
    h32 = h.astype(F32)

    ph = _dot(h, wh_ref[...])
    hq_ref[...] = ph[:, 0 * HG_WIDTH:1 * HG_WIDTH]
    hf_ref[...] = ph[:, 1 * HG_WIDTH:2 * HG_WIDTH]
    hi_ref[...] = ph[:, 2 * HG_WIDTH:3 * HG_WIDTH]
    hg_ref[...] = ph[:, 3 * HG_WIDTH:4 * HG_WIDTH]
    q = _dot(h, wq_ref[...]) * (FOX_HEAD_DIM ** -0.5)
    pg = _dot(h, wg_ref[...])
    ga_ref[...] = pg[:, 0:D_MODEL]
    gb_ref[...] = pg[:, D_MODEL:2 * D_MODEL]

    if not prompt:
        fq_ref[...] = q
        k_ref[...] = _dot(h, wk_ref[...])
        v_ref[...] = _dot(h, wv_ref[...])
        lf_ref[...] = _log_sigmoid(_dot(h32, wff_ref[...]) + bff_ref[...])
    else:
        kvf = _dot(h, wk_ref[...])
        kt = kvf[:, 0:FOX_WIDTH].T
        vt = kvf[:, FOX_WIDTH:2 * FOX_WIDTH].T
        k_ref[0, 0] = kt
        v_ref[0, 0] = vt
        ff = kvf[:, 2 * FOX_WIDTH:2 * FOX_WIDTH + LANES]
        lf = _log_sigmoid(ff[:, 0:FOX_HEADS] + bff_ref[...])
        lft = _log_sigmoid(ff.T[0:FOX_HEADS, :] + bfft_ref[...])
        lf_ref[0, 0] = lft

        @pl.when(pl.program_id(0) % tiles_per_seq == 0)
        def _():
            carry_c[...] = jnp.zeros_like(carry_c)
            carry_r[...] = jnp.zeros_like(carry_r)

        fcol = _dot_sel_lhs(tril_ref[...], lf) + carry_c[...]
        frow = _dot_sel_rhs(lft, triu_ref[...]) + carry_r[...]
        tm = lf.shape[0]
        carry_c[...] = fcol[tm - 1:tm, :]
        carry_r[...] = frow[:, tm - 1:tm]

        low = lax.broadcasted_iota(jnp.int32, (1, LANES), 1) < FOX_HEAD_DIM
        q_aug = _dot(jnp.concatenate(list(_split3(fcol)) + [jnp.ones((tm, FOX_HEADS), F32)], axis=1),
                     plq_ref[...])
        k_aug = _dot(plk_ref[...],
                     jnp.concatenate([-r for r in _split3(frow)] + [jnp.ones((FOX_HEADS, tm), F32)], axis=0))
        ones = jnp.ones((FOX_HEAD_DIM, tm), F32)
        for hd in range(FOX_HEADS):
            feat = slice(hd * FOX_HEAD_DIM, (hd + 1) * FOX_HEAD_DIM)
            blk = slice(hd * LANES, (hd + 1) * LANES)
            src = q[:, (hd // 2) * LANES:(hd // 2 + 1) * LANES]
            if hd % 2 == 0:
                fq_ref[:, blk] = jnp.where(low, src, q_aug[:, blk]).astype(BF16)
                ka_ref[0, blk, :] = jnp.concatenate([kt[feat, :], k_aug[feat, :]], axis=0).astype(BF16)
                va_ref[0, blk, :] = jnp.concatenate([vt[feat, :], ones], axis=0).astype(BF16)
            else:
                fq_ref[:, blk] = jnp.where(low, q_aug[:, blk], src).astype(BF16)
                ka_ref[0, blk, :] = jnp.concatenate([k_aug[feat, :], kt[feat, :]], axis=0).astype(BF16)
                va_ref[0, blk, :] = jnp.concatenate([ones, vt[feat, :]], axis=0).astype(BF16)


def _bias_placements():
    plq = np.zeros((4 * FOX_HEADS, FOX_HEADS * LANES), np.float32)
    plk = np.zeros((FOX_HEADS * FOX_HEAD_DIM, 4 * FOX_HEADS), np.float32)
    for hd in range(FOX_HEADS):
        spare = hd * LANES + (FOX_HEAD_DIM if hd % 2 == 0 else 0)
        for j in range(3):
            plq[j * FOX_HEADS + hd, spare + j] = 1.0
            plq[3 * FOX_HEADS + hd, spare + 3 + j] = 1.0
            plk[hd * FOX_HEAD_DIM + j, 3 * FOX_HEADS + hd] = 1.0
            plk[hd * FOX_HEAD_DIM + 3 + j, j * FOX_HEADS + hd] = 1.0
    return jnp.asarray(plq), jnp.asarray(plk)


def _mixin_call(layer, x_in, f_in, prev_mod, ln_g, ln_b, mod, w, n_seq, seq_len, tm, prompt, kv_bufs):
    T = x_in.shape[0]
    tiles_per_seq = max(seq_len // tm, 1)
    tok = lambda width: pl.BlockSpec((tm, width), lambda i: (i, 0))
    has_prev = f_in is not None

    args, specs = [x_in], [tok(D_MODEL)]
    if has_prev:
        args += [f_in, prev_mod]
        specs += [tok(D_MODEL), _mod_spec(prev_mod, 5, tm, tiles_per_seq)]
    args += [ln_g.reshape(1, D_MODEL), ln_b.reshape(1, D_MODEL), mod, mod]
    specs += [_const_spec((1, D_MODEL)), _const_spec((1, D_MODEL)),
              _mod_spec(mod, 0, tm, tiles_per_seq), _mod_spec(mod, 1, tm, tiles_per_seq)]
    if prompt:
        wnames = ("wh", "wq", "wkvf", "wff", "wff", "wg", "bff", "bfft")
    else:
        wnames = ("wh", "wq", "wk", "wv", "wff", "wg", "bff", "bfft")
    args += [w[n] for n in wnames]
    specs += [_const_spec(w[n].shape) for n in wnames]
    if prompt:
        plq, plk = _bias_placements()
        args += [_tri(tm, True), _tri(tm, False), plq, plk]
        specs += [_const_spec((tm, tm)), _const_spec((tm, tm)), _const_spec(plq.shape), _const_spec(plk.shape)]

    names = ["x", "hq", "hf", "hi", "hg", "fq", "ga", "gb", "k", "v", "lf"]
    widths = [D_MODEL, HG_WIDTH, HG_WIDTH, HG_WIDTH, HG_WIDTH,
              FOX_HEADS * LANES if prompt else FOX_WIDTH, D_MODEL, D_MODEL]
    dtypes = [F32] * 5 + [BF16 if prompt else F32] + [F32] * 2
    out_shape = [jax.ShapeDtypeStruct((T, wd), dt) for wd, dt in zip(widths, dtypes)]
    out_specs = [tok(wd) for wd in widths]
    scratch = []
    aliases = {}
    if prompt:
        fm = lambda rows: pl.BlockSpec((1, 1, rows, tm),
                                       lambda i: (i // tiles_per_seq, layer, 0, i % tiles_per_seq))
        out_shape += [jax.ShapeDtypeStruct((n_seq, DEPTH, FOX_WIDTH, seq_len), F32),
                      jax.ShapeDtypeStruct((n_seq, DEPTH, FOX_WIDTH, seq_len), F32),
                      jax.ShapeDtypeStruct((n_seq, DEPTH, FOX_HEADS, seq_len), F32)]
        out_specs += [fm(FOX_WIDTH), fm(FOX_WIDTH), fm(FOX_HEADS)]
        if kv_bufs is not None:
            for j, buf in enumerate(kv_bufs):
                aliases[len(args)] = 8 + j
                args.append(buf)
                specs.append(pl.BlockSpec(memory_space=pl.ANY))
        names += ["ka", "va"]
        aug = jax.ShapeDtypeStruct((n_seq, FOX_HEADS * LANES, seq_len), BF16)
        aug_spec = pl.BlockSpec((1, FOX_HEADS * LANES, tm),
                                lambda i: (i // tiles_per_seq, 0, i % tiles_per_seq))
        out_shape += [aug, aug]
        out_specs += [aug_spec, aug_spec]
        scratch = [pltpu.VMEM((1, FOX_HEADS), F32), pltpu.VMEM((FOX_HEADS, 1), F32)]
    else:
        out_shape += [jax.ShapeDtypeStruct((T, FOX_WIDTH), F32), jax.ShapeDtypeStruct((T, FOX_WIDTH), F32),
                      jax.ShapeDtypeStruct((T, FOX_HEADS), F32)]
        out_specs += [tok(FOX_WIDTH), tok(FOX_WIDTH), tok(FOX_HEADS)]

    outs = pl.pallas_call(
        functools.partial(_mixin_kernel, has_prev, prompt, len(aliases), tiles_per_seq),
        grid=(T // tm,),
        in_specs=specs,
        out_specs=out_specs,
        out_shape=out_shape,
        scratch_shapes=scratch,
        input_output_aliases=aliases,
        compiler_params=_cparams(("arbitrary",)),
        name="mixer_in",
    )(*args)
    return dict(zip(names, outs))


def _cumsum_rows(x):
    n = x.shape[0]
    row = lax.broadcasted_iota(jnp.int32, x.shape, 0)
    shift = 1
    while shift < n:
        x = x + jnp.where(row >= shift, pltpu.roll(x, shift, 0), 0.0)
        shift *= 2
    return x


def _hgrn_kernel(layer, chunk, sub, n_chunks, hq_ref, hf_ref, hi_ref, hg_ref, lb_ref, ng_ref, s0_ref,
                 o_ref, sout_ref, st_ref):
    t = pl.program_id(1)
    mm = BF16 if chunk >= 16 else F32

    @pl.when(t == 0)
    def _():
        for hd in range(HG_HEADS):
            st_ref[hd] = s0_ref[0, hd].T

    lb_all = lb_ref[...]
    e = jnp.exp(lb_all - jnp.max(lb_all, axis=0, keepdims=True))
    p = e / jnp.sum(e, axis=0, keepdims=True)
    lb = jnp.zeros((1, HG_WIDTH), F32)
    for j in range(1, layer + 1):
        lb = lb + p[j:j + 1, :]
    log_lb = jnp.log(jnp.maximum(lb, LB_FLOOR))
    log1m_lb = jnp.log1p(-lb)
    n_sub = chunk // sub
    causal = (lax.broadcasted_iota(jnp.int32, (chunk, chunk), 1)
              <= lax.broadcasted_iota(jnp.int32, (chunk, chunk), 0))

    def chunk_body(ci, carry):
        r0 = pl.multiple_of(ci * chunk, chunk)
        for hd in range(HG_HEADS):
            ls = slice(hd * HG_DK, (hd + 1) * HG_DK)
            z = hf_ref[pl.ds(r0, chunk), ls]
            q = _silu(hq_ref[pl.ds(r0, chunk), ls])
            v = hi_ref[pl.ds(r0, chunk), ls].astype(mm)
            a = log_lb[:, ls]
            b = log1m_lb[:, ls] + _log_sigmoid(z)
            log_f = jnp.maximum(a, b) + jnp.log1p(jnp.exp(-jnp.abs(a - b)))
            k = (1.0 - lb[:, ls]) * jax.nn.sigmoid(-z)
            cum = _cumsum_rows(log_f)
            blocks = []
            for bi in range(n_sub):
                rows = slice(bi * sub, (bi + 1) * sub)
                base = jnp.zeros((1, HG_DK), F32) if bi == 0 else cum[bi * sub - 1:bi * sub, :]
                k_i = (k * jnp.exp(jnp.minimum(base - cum, EXP_CLAMP))).astype(mm)
                q_i = (q[rows, :] * jnp.exp(cum[rows, :] - base)).astype(mm)
                blocks.append(_dot(q_i, k_i, NT_DIMS))
            attn = blocks[0] if n_sub == 1 else jnp.concatenate(blocks, axis=0)
            attn = jnp.where(causal, attn, 0.0).astype(mm)
            s_t = st_ref[hd]
            o = _dot(attn, v) + _dot((q * jnp.exp(cum)).astype(mm), s_t.astype(mm), NT_DIMS)
            cum_end = cum[chunk - 1:chunk, :]
            k_end = (k * jnp.exp(cum_end - cum)).astype(mm)
            st_ref[hd] = jnp.exp(cum_end) * s_t + _dot(v, k_end, TN_DIMS)
            o = o * lax.rsqrt(jnp.mean(o * o, axis=-1, keepdims=True) + RMS_EPS)
            o_ref[pl.ds(r0, chunk), ls] = o * ng_ref[:, ls] * _silu(hg_ref[pl.ds(r0, chunk), ls])
        return carry

    lax.fori_loop(0, n_chunks, chunk_body, 0, unroll=min(8, n_chunks))

    @pl.when(t == pl.num_programs(1) - 1)
    def _():
        for hd in range(HG_HEADS):
            sout_ref[0, hd] = st_ref[hd].T


def _hgrn_call(layer, p, lower_bounds, norm_g, s0, n_seq, seq_len):
    chunk = HG_CHUNK if seq_len % HG_CHUNK == 0 else seq_len
    sub = min(HG_SUB, chunk)
    tb = min(seq_len, 512)
    nt = seq_len // tb
    tok = pl.BlockSpec((tb, HG_WIDTH), lambda b, t: (b * nt + t, 0))
    st_spec = pl.BlockSpec((1, HG_HEADS, HG_DK, HG_DK), lambda b, t: (b, 0, 0, 0))
    return pl.pallas_call(
        functools.partial(_hgrn_kernel, layer, chunk, sub, tb // chunk),
        grid=(n_seq, nt),
        in_specs=[tok, tok, tok, tok,
                  pl.BlockSpec((DEPTH, HG_WIDTH), lambda b, t: (0, 0)),
                  pl.BlockSpec((1, HG_WIDTH), lambda b, t: (0, 0)),
                  st_spec],
        out_specs=[tok, st_spec],
        out_shape=[jax.ShapeDtypeStruct((n_seq * seq_len, HG_WIDTH), F32),
                   jax.ShapeDtypeStruct((n_seq, HG_HEADS, HG_DK, HG_DK), F32)],
        scratch_shapes=[pltpu.VMEM((HG_HEADS, HG_DK, HG_DK), F32)],
        compiler_params=_cparams(("arbitrary", "arbitrary")),
        name="hgrn",
    )(p["hq"], p["hf"], p["hi"], p["hg"], lower_bounds, norm_g.reshape(1, HG_WIDTH), s0)


def _fox_kernel(tq, tk, qi_ref, ki_ref, q_ref, k_ref, v_ref, o_ref, m_ref, acc_ref):
    qi = qi_ref[pl.program_id(1)]
    ki = ki_ref[pl.program_id(1)]

    @pl.when(ki == 0)
    def _():
        m_ref[...] = jnp.full_like(m_ref, MASK_VALUE)
        acc_ref[...] = jnp.zeros_like(acc_ref)

    def absorb(diagonal):
        if diagonal:
            visible = (lax.broadcasted_iota(jnp.int32, (tq, tk), 1)
                       <= lax.broadcasted_iota(jnp.int32, (tq, tk), 0))
        for hd in range(FOX_HEADS):
            blk = slice(hd * LANES, (hd + 1) * LANES)
            s = _dot(q_ref[:, blk], k_ref[0, blk, :])
            if diagonal:
                s = jnp.where(visible, s, MASK_VALUE)
            cols = [s[:, c * LANES:(c + 1) * LANES] for c in range(tk // LANES)]
            cmax = cols[0]
            for c in cols[1:]:
                cmax = jnp.maximum(cmax, c)
            m_old = m_ref[hd]
            m_new = jnp.maximum(m_old, jnp.broadcast_to(jnp.max(cmax, axis=-1, keepdims=True), (tq, LANES)))
            pe = jnp.concatenate([jnp.exp(c - m_new) for c in cols], axis=1).astype(BF16)
            m_ref[hd] = m_new
            acc_ref[:, blk] = jnp.exp(m_old - m_new) * acc_ref[:, blk] + _dot(pe, v_ref[0, blk, :], NT_DIMS)

    @pl.when(ki < qi)
    def _():
        absorb(False)

    @pl.when(ki == qi)
    def _():
        absorb(True)
        low = lax.broadcasted_iota(jnp.int32, (1, LANES), 1) < FOX_HEAD_DIM
        for pr in range(FOX_HEADS // 2):
            even = acc_ref[:, 2 * pr * LANES:(2 * pr + 1) * LANES]
            odd = acc_ref[:, (2 * pr + 1) * LANES:(2 * pr + 2) * LANES]
            o_ref[:, pr * LANES:(pr + 1) * LANES] = jnp.where(
                low, even / pltpu.roll(even, FOX_HEAD_DIM, 1), odd / pltpu.roll(odd, FOX_HEAD_DIM, 1))


def _fox_call(p, n_seq, seq_len, tq):
    tk = tq
    nq = seq_len // tq
    T = n_seq * seq_len
    width = FOX_HEADS * LANES
    pairs = [(qi, ki) for qi in range(nq) for ki in range(qi + 1)]
    qi_tab = jnp.asarray([pr[0] for pr in pairs], jnp.int32)
    ki_tab = jnp.asarray([pr[1] for pr in pairs], jnp.int32)
    kspec = pl.BlockSpec((1, width, tk), lambda b, s, qt, kt: (b, 0, kt[s]))
    grid_spec = pltpu.PrefetchScalarGridSpec(
        num_scalar_prefetch=2,
        grid=(n_seq, len(pairs)),
        in_specs=[pl.BlockSpec((tq, width), lambda b, s, qt, kt: (b * nq + qt[s], 0)), kspec, kspec],
        out_specs=pl.BlockSpec((tq, FOX_WIDTH), lambda b, s, qt, kt: (b * nq + qt[s], 0)),
        scratch_shapes=[pltpu.VMEM((FOX_HEADS, tq, LANES), F32), pltpu.VMEM((tq, width), F32)],
    )
    return pl.pallas_call(
        functools.partial(_fox_kernel, tq, tk),
        grid_spec=grid_spec,
        out_shape=jax.ShapeDtypeStruct((T, FOX_WIDTH), F32),
        compiler_params=_cparams(("arbitrary", "arbitrary")),
        name="fox_prompt",
    )(qi_tab, ki_tab, p["fq"], p["ka"], p["va"])


PAGES_PER_STEP = 16


def _decode_kernel(n_new, pt_ref, q_ref, kn_ref, vn_ref, lfn_ref, *refs):
    npg = PAGES_PER_STEP
    k_refs = refs[0:npg]
    v_refs = refs[npg:2 * npg]
    lf_refs = refs[2 * npg:3 * npg]
    o_ref, m_ref, l_ref, acc_ref, carry_ref = refs[3 * npg:]
    j = pl.program_id(1)
    rows = FOX_HEADS * n_new

    @pl.when(j == 0)
    def _():
        m_ref[...] = jnp.full_like(m_ref, MASK_VALUE)
        l_ref[...] = jnp.zeros_like(l_ref)
        acc_ref[...] = jnp.zeros_like(acc_ref)
        carry_ref[...] = jnp.zeros_like(carry_ref)

    q = q_ref[...]
    rr = lax.broadcasted_iota(jnp.int32, (rows, FOX_WIDTH), 0)
    cc = lax.broadcasted_iota(jnp.int32, (rows, FOX_WIDTH), 1)
    q_rep = jnp.concatenate([q] * FOX_HEADS, axis=0)
    qbd = jnp.where((rr // n_new) == (cc // FOX_HEAD_DIM), q_rep, 0.0)
    qbd_bf = qbd.astype(BF16)
    ehe = jnp.where(lax.broadcasted_iota(jnp.int32, (rows, FOX_HEADS), 0) // n_new
                    == lax.broadcasted_iota(jnp.int32, (rows, FOX_HEADS), 1), 1.0, 0.0)
    triu = _tri(PAGE_SIZE, False)

    def absorb(s, pv_fn):
        m_old = m_ref[...]
        m_new = jnp.maximum(m_old, jnp.max(s, axis=-1, keepdims=True))
        alpha = jnp.exp(m_old - m_new)
        pe = jnp.exp(s - m_new)
        l_ref[...] = alpha * l_ref[...] + jnp.sum(pe, axis=-1, keepdims=True)
        m_ref[...] = m_new
        acc_ref[...] = alpha * acc_ref[...] + pv_fn(pe)

    lf_all = jnp.concatenate([lf_refs[pg][...] for pg in range(npg)], axis=0)
    within = _dot_sel_rhs(lf_all, triu)
    nr = npg * FOX_HEADS
    ri = lax.broadcasted_iota(jnp.int32, (nr, nr), 0)
    ci = lax.broadcasted_iota(jnp.int32, (nr, nr), 1)
    earlier = jnp.where(jnp.logical_and(ri % FOX_HEADS == ci % FOX_HEADS, ci < ri), 1.0, 0.0)
    totals = jnp.broadcast_to(within[:, PAGE_SIZE - 1:PAGE_SIZE], (nr, PAGE_SIZE))
    carry_rep = jnp.concatenate([jnp.broadcast_to(carry_ref[...], (FOX_HEADS, PAGE_SIZE))] * npg, axis=0)
    f_all = within + _dot_sel_lhs(earlier, totals) + carry_rep
    carry_ref[...] = f_all[nr - FOX_HEADS:nr, PAGE_SIZE - 1:PAGE_SIZE]
    bias = jnp.concatenate(
        [jnp.concatenate([jnp.broadcast_to(f_all[pg * FOX_HEADS + hd:pg * FOX_HEADS + hd + 1, :],
                                           (n_new, PAGE_SIZE)) for hd in range(FOX_HEADS)], axis=0)
         for pg in range(npg)], axis=1)
    kt = jnp.concatenate([k_refs[pg][...].astype(BF16) for pg in range(npg)], axis=1)
    vt = jnp.concatenate([v_refs[pg][...].astype(BF16) for pg in range(npg)], axis=1)
    absorb(_dot(qbd_bf, kt) - bias, lambda pe: _dot(pe.astype(BF16), vt, NT_DIMS))

    @pl.when(j == pl.num_programs(1) - 1)
    def _():
        kn = kn_ref[...]
        vn = vn_ref[...]
        eye = (lax.broadcasted_iota(jnp.int32, (FOX_HEADS, FOX_HEADS), 0)
               == lax.broadcasted_iota(jnp.int32, (FOX_HEADS, FOX_HEADS), 1))
        carry_row = jnp.sum(jnp.where(eye, carry_ref[...], 0.0), axis=0, keepdims=True)
        f_new = _dot_sel_lhs(_tri(n_new, True), lfn_ref[...]) + carry_row
        s = _dot(qbd, kn, NT_DIMS) - _dot_sel_lhs(ehe, f_new, NT_DIMS)
        key_i = lax.broadcasted_iota(jnp.int32, (rows, n_new), 1)
        qry_i = lax.broadcasted_iota(jnp.int32, (rows, n_new), 0) % n_new
        s = jnp.where(key_i <= qry_i, s, MASK_VALUE)
        absorb(s, lambda pe: _dot(pe, vn))
        out = acc_ref[...] / l_ref[...]
        lane_head = lax.broadcasted_iota(jnp.int32, (n_new, FOX_WIDTH), 1) // FOX_HEAD_DIM
        res = jnp.zeros((n_new, FOX_WIDTH), F32)
        for hd in range(FOX_HEADS):
            res = res + jnp.where(lane_head == hd, out[hd * n_new:(hd + 1) * n_new, :], 0.0)
        o_ref[...] = res


def _decode_call(layer, p, cache_kt, cache_vt, cache_lft, page_table, n_seq, n_new):
    n_pages = page_table.shape[1]
    npg = PAGES_PER_STEP
    pt_flat = page_table.reshape(-1).astype(jnp.int32)

    def page_map(pg):
        return lambda n, j, pt: (pt[n * n_pages + j * npg + pg], layer, 0, 0)

    new_w = pl.BlockSpec((n_new, FOX_WIDTH), lambda n, j, pt: (n, 0))
    in_specs = [new_w, new_w, new_w, pl.BlockSpec((n_new, FOX_HEADS), lambda n, j, pt: (n, 0))]
    in_specs += [pl.BlockSpec((None, None, FOX_WIDTH, PAGE_SIZE), page_map(pg)) for pg in range(npg)]
    in_specs += [pl.BlockSpec((None, None, FOX_WIDTH, PAGE_SIZE), page_map(pg)) for pg in range(npg)]
    in_specs += [pl.BlockSpec((None, None, FOX_HEADS, PAGE_SIZE), page_map(pg)) for pg in range(npg)]
    rows = FOX_HEADS * n_new
    grid_spec = pltpu.PrefetchScalarGridSpec(
        num_scalar_prefetch=1,
        grid=(n_seq, n_pages // npg),
        in_specs=in_specs,
        out_specs=new_w,
        scratch_shapes=[pltpu.VMEM((rows, 1), F32), pltpu.VMEM((rows, 1), F32),
                        pltpu.VMEM((rows, FOX_WIDTH), F32), pltpu.VMEM((FOX_HEADS, 1), F32)],
    )
    return pl.pallas_call(
        functools.partial(_decode_kernel, n_new),
        grid_spec=grid_spec,
        out_shape=jax.ShapeDtypeStruct((n_seq * n_new, FOX_WIDTH), F32),
        compiler_params=_cparams(("arbitrary", "arbitrary")),
        name="fox_decode",
    )(pt_flat, p["fq"], p["k"], p["v"], p["lf"],
      *([cache_kt] * npg), *([cache_vt] * npg), *([cache_lft] * npg))


def _merge_kernel(tiles_per_win, part, oa_ref, ob_ref, ga_ref, gb_ref, x_ref, g1_ref, sh_ref, sc_ref,
                  wa_ref, wb_ref, wo_ref, lng_ref, lnb_ref, wrt_ref, br_ref, triu_ref,
                  x1_ref, h2_ref, cw_ref, route_ref, carry_ref):
    @pl.when(pl.program_id(0) % tiles_per_win == 0)
    def _():
        carry_ref[...] = jnp.zeros_like(carry_ref)

    for pi in range(x_ref.shape[0] // part):
        _merge_part(slice(pi * part, (pi + 1) * part), oa_ref, ob_ref, ga_ref, gb_ref, x_ref, g1_ref,
                    sh_ref, sc_ref, wa_ref, wb_ref, wo_ref, lng_ref, lnb_ref, wrt_ref, br_ref, triu_ref,
                    x1_ref, h2_ref, cw_ref, route_ref, carry_ref)


def _merge_part(rs, oa_ref, ob_ref, ga_ref, gb_ref, x_ref, g1_ref, sh_ref, sc_ref,
                wa_ref, wb_ref, wo_ref, lng_ref, lnb_ref, wrt_ref, br_ref, triu_ref,
                x1_ref, h2_ref, cw_ref, route_ref, carry_ref):
    def mod_rows(ref):
        return ref[0, 0] if ref.shape[2] == 1 else ref[0, 0, rs, :]

    ya = _dot(oa_ref[rs, :].astype(BF16), wa_ref[...])
    yb = _dot(ob_ref[rs, :].astype(BF16), wb_ref[...])
    merged = jax.nn.sigmoid(ga_ref[rs, :]) * ya + jax.nn.sigmoid(gb_ref[rs, :]) * yb
    m = _dot(merged.astype(BF16), wo_ref[...])
    x1 = _layer_norm(DEEPNORM_ALPHA * x_ref[rs, :] + mod_rows(g1_ref) * m, lng_ref[...], lnb_ref[...])
    x1_ref[rs, :] = x1
    h2 = x1 * (1.0 + mod_rows(sc_ref)) + mod_rows(sh_ref)
    h2_ref[rs, :] = h2.astype(BF16)

    tm = h2.shape[0]
    hp = [t.astype(BF16) for t in _split3(h2)]
    wr = [wrt_ref[j] for j in range(3)]
    small = _dot(hp[1], wr[1]) + _dot(hp[0], wr[2]) + _dot(hp[2], wr[0])
    mid = _dot(hp[0], wr[1]) + _dot(hp[1], wr[0])
    logits = (small + mid + _dot(hp[0], wr[0])).T[:N_EXPERTS, :] + br_ref[...]
    ex = jnp.exp(logits - jnp.max(logits, axis=0, keepdims=True))
    scores = ex / jnp.sum(ex, axis=0, keepdims=True)
    gs = []
    for g in range(N_GROUPS):
        r = [scores[g * EXPERTS_PER_GROUP + e:g * EXPERTS_PER_GROUP + e + 1, :]
             for e in range(EXPERTS_PER_GROUP)]
        best = r[0] + r[1]
        for a in range(EXPERTS_PER_GROUP):
            for b in range(a + 1, EXPERTS_PER_GROUP):
                best = jnp.maximum(best, r[a] + r[b])
        gs.append(best)
    gmax = jnp.maximum(jnp.maximum(gs[0], gs[1]), jnp.maximum(gs[2], gs[3]))
    gid = jnp.where(gs[0] == gmax, 0, jnp.where(gs[1] == gmax, 1, jnp.where(gs[2] == gmax, 2, 3)))
    erow = lax.broadcasted_iota(jnp.int32, (N_EXPERTS, tm), 0)
    masked = jnp.where(erow // EXPERTS_PER_GROUP == gid, scores, -1.0)
    top1 = jnp.max(masked, axis=0, keepdims=True)
    idx1 = jnp.min(jnp.where(masked == top1, erow, N_EXPERTS), axis=0, keepdims=True)
    masked2 = jnp.where(erow == idx1, -2.0, masked)
    top2 = jnp.max(masked2, axis=0, keepdims=True)
    idx2 = jnp.min(jnp.where(masked2 == top2, erow, N_EXPERTS), axis=0, keepdims=True)
    den = top1 + top2
    cw_ref[:, rs] = jnp.where(erow == idx1, top1 / den, 0.0) + jnp.where(erow == idx2, top2 / den, 0.0)

    grow = lax.broadcasted_iota(jnp.int32, (8, tm), 0)
    member = grow == gid
    incl = _dot(jnp.where(member, 1.0, 0.0), triu_ref[...]) + carry_ref[...]
    carry_ref[...] = incl[:, tm - 1:tm]
    rank = jnp.sum(jnp.where(member, incl, 0.0), axis=0, keepdims=True) - 1.0
    route_ref[:, rs] = jnp.where(grow == 0, gid, jnp.where(grow == 1, rank.astype(jnp.int32), 0))


def _merge_call(o_a, o_b, p, mod, w, ln_g, ln_b, w_router_t, b_router, seq_len, tm, win):
    T = o_a.shape[0]
    part = min(tm, 256)
    tiles_per_seq = max(seq_len // tm, 1)
    tok = lambda width: pl.BlockSpec((tm, width), lambda i: (i, 0))
    return pl.pallas_call(
        functools.partial(_merge_kernel, win // tm, part),
        grid=(T // tm,),
        in_specs=[tok(HG_WIDTH), tok(FOX_WIDTH), tok(D_MODEL), tok(D_MODEL), tok(D_MODEL),
                  _mod_spec(mod, 2, tm, tiles_per_seq), _mod_spec(mod, 3, tm, tiles_per_seq),
                  _mod_spec(mod, 4, tm, tiles_per_seq),
                  _const_spec((HG_WIDTH, D_MODEL)), _const_spec((FOX_WIDTH, D_MODEL)),
                  _const_spec((D_MODEL, D_MODEL)), _const_spec((1, D_MODEL)), _const_spec((1, D_MODEL)),
                  _const_spec((3, D_MODEL, LANES)), _const_spec((N_EXPERTS, 1)), _const_spec((part, part))],
        out_specs=[tok(D_MODEL), tok(D_MODEL),
                   pl.BlockSpec((N_EXPERTS, tm), lambda i: (0, i)),
                   pl.BlockSpec((8, tm), lambda i: (0, i))],
        out_shape=[jax.ShapeDtypeStruct((T, D_MODEL), F32), jax.ShapeDtypeStruct((T, D_MODEL), BF16),
                   jax.ShapeDtypeStruct((N_EXPERTS, T), F32), jax.ShapeDtypeStruct((8, T), jnp.int32)],
        scratch_shapes=[pltpu.VMEM((8, 1), F32)],
        compiler_params=_cparams(("arbitrary",)),
        name="merge",
    )(o_a, o_b, p["ga"], p["gb"], p["x"], mod, mod, mod, w["wa"], w["wb"], w["wo"],
      ln_g.reshape(1, D_MODEL), ln_b.reshape(1, D_MODEL), w_router_t, b_router.reshape(N_EXPERTS, 1),
      _tri(part, False))


MOE_UNIT = 128
MOE_MAX_UNITS = 4


def _moe_kernel(h_ref, cw_ref, route_ref, wg_ref, wu_ref, wd_ref, y_ref):
    g = pl.program_id(1)
    win = h_ref.shape[0]

    @pl.when(g == 0)
    def _():
        y_ref[...] = jnp.zeros_like(y_ref)

    gid = route_ref[0:1, :]
    rank = route_ref[1:2, :]
    member = gid == g
    count = jnp.sum(jnp.where(member, 1, 0))

    def process(base, rows):
        slot = lax.broadcasted_iota(jnp.int32, (rows, win), 0)
        ecol = lax.broadcasted_iota(jnp.int32, (rows, N_EXPERTS), 1)
        perm_f = jnp.where(jnp.logical_and(member, rank - base == slot), 1.0, 0.0)
        perm = perm_f.astype(BF16)
        hs = _dot(perm, h_ref[...]).astype(BF16)
        cwg = _dot_sel_lhs(perm_f, cw_ref[...], NT_DIMS)
        acc = jnp.zeros((rows, D_MODEL), F32)
        for e in range(EXPERTS_PER_GROUP):
            cwe = jnp.sum(jnp.where(ecol == g * EXPERTS_PER_GROUP + e, cwg, 0.0), axis=1, keepdims=True)
            a = _dot(hs, wg_ref[0, e])
            u = _dot(hs, wu_ref[0, e])
            hid = (_silu(a) * u * cwe).astype(BF16)
            acc = acc + _dot(hid, wd_ref[0, e])
        y_ref[...] += _dot(perm, acc.astype(BF16), TN_DIMS)

    big = MOE_UNIT * MOE_MAX_UNITS
    n_big = count // big

    def big_pass(j, carry):
        process(j * big, big)
        return carry

    lax.fori_loop(0, n_big, big_pass, 0)
    rest_units = (count - n_big * big + MOE_UNIT - 1) // MOE_UNIT
    for units in range(1, MOE_MAX_UNITS + 1):
        @pl.when(rest_units == units)
        def _(units=units):
            process(n_big * big, units * MOE_UNIT)


def _moe_call(h2, cw, route, w, win):
    T = h2.shape[0]
    wspec = lambda shp: pl.BlockSpec((1,) + shp, lambda i, g: (g, 0, 0, 0))
    return pl.pallas_call(
        _moe_kernel,
        grid=(T // win, N_GROUPS),
        in_specs=[pl.BlockSpec((win, D_MODEL), lambda i, g: (i, 0)),
                  pl.BlockSpec((N_EXPERTS, win), lambda i, g: (0, i)),
                  pl.BlockSpec((8, win), lambda i, g: (0, i)),
                  wspec((EXPERTS_PER_GROUP, D_MODEL, D_EXPERT)),
                  wspec((EXPERTS_PER_GROUP, D_MODEL, D_EXPERT)),
                  wspec((EXPERTS_PER_GROUP, D_EXPERT, D_MODEL))],
        out_specs=pl.BlockSpec((win, D_MODEL), lambda i, g: (i, 0)),
        out_shape=jax.ShapeDtypeStruct((T, D_MODEL), F32),
        compiler_params=_cparams(("arbitrary", "arbitrary")),
        name="moe",
    )(h2, cw, route, w["eg"], w["eu"], w["ed"])


def _final_kernel(x_ref, f_ref, g2_ref, lng_ref, lnb_ref, o_ref):
    o_ref[...] = _layer_norm(DEEPNORM_ALPHA * x_ref[...] + g2_ref[0, 0] * f_ref[...],
                             lng_ref[...], lnb_ref[...])


def _final_call(x1, f, mod, ln_g, ln_b, seq_len, tm):
    T = x1.shape[0]
    tok = pl.BlockSpec((tm, D_MODEL), lambda i: (i, 0))
    return pl.pallas_call(
        _final_kernel,
        grid=(T // tm,),
        in_specs=[tok, tok, _mod_spec(mod, 5, tm, max(seq_len // tm, 1)),
                  _const_spec((1, D_MODEL)), _const_spec((1, D_MODEL))],
        out_specs=tok,
        out_shape=jax.ShapeDtypeStruct((T, D_MODEL), F32),
        compiler_params=_cparams(("arbitrary",)),
        name="final_norm",
    )(x1, f, mod, ln_g.reshape(1, D_MODEL), ln_b.reshape(1, D_MODEL))


def _layer_weights(prm, l):
    c0 = 4 * HG_WIDTH
    c1 = c0 + FOX_WIDTH
    c2 = c1 + FOX_WIDTH
    c3 = c2 + FOX_WIDTH
    c4 = c3 + FOX_HEADS
    w = prm["w_in"][l]
    grouped = lambda a, shp: a.astype(BF16).reshape((N_GROUPS, EXPERTS_PER_GROUP) + shp)
    return {
        "wh": w[:, :c0].astype(BF16),
        "wq": w[:, c0:c1].astype(BF16),
        "wk": w[:, c1:c2].astype(BF16),
        "wv": w[:, c2:c3].astype(BF16),
        "wff": w[:, c3:c4].astype(BF16).astype(F32),
        "wkvf": jnp.concatenate([w[:, c1:c4].astype(BF16),
                                 jnp.zeros((D_MODEL, LANES - FOX_HEADS), BF16)], axis=1),
        "wg": w[:, c4:].astype(BF16),
        "bff": prm["b_fox_f"][l].reshape(1, FOX_HEADS),
        "bfft": prm["b_fox_f"][l].reshape(FOX_HEADS, 1),
        "wa": prm["w_branch_a"][l].astype(BF16),
        "wb": prm["w_branch_b"][l].astype(BF16),
        "wo": prm["w_out"][l].astype(BF16),
        "eg": grouped(prm["w_exp_gate"][l], (D_MODEL, D_EXPERT)),
        "eu": grouped(prm["w_exp_up"][l], (D_MODEL, D_EXPERT)),
        "ed": grouped(prm["w_exp_down"][l], (D_EXPERT, D_MODEL)),
    }


def _trunk(x, mods, hg_state, paged, prm, weights):
    n_seq, seq_len, _ = x.shape
    T = n_seq * seq_len
    prompt = paged is None
    tm = min(256, T)
    win = min(1024, T)
    wr = jnp.pad(prm["w_router"], ((0, 0), (0, LANES - N_EXPERTS)))
    wr1 = wr.astype(BF16)
    wr2 = (wr - wr1.astype(F32)).astype(BF16)
    wr3 = (wr - wr1.astype(F32) - wr2.astype(F32)).astype(BF16)
    w_router_t = jnp.stack([wr1, wr2, wr3])
    if prompt:
        s0_all = jnp.zeros((DEPTH, n_seq, HG_HEADS, HG_DK, HG_DK), F32)
    else:
        s0_all = hg_state.astype(F32)
        cache_k, cache_v, cache_lf, page_table = paged
        n_phys = cache_k.shape[0]
        cache_kt = jnp.transpose(cache_k, (0, 2, 3, 4, 1)).reshape(n_phys, DEPTH, FOX_WIDTH, PAGE_SIZE)
        cache_vt = jnp.transpose(cache_v, (0, 2, 3, 4, 1)).reshape(n_phys, DEPTH, FOX_WIDTH, PAGE_SIZE)
        cache_lft = jnp.transpose(cache_lf, (0, 2, 3, 1))

    ks, vs, lfs, states = [], [], [], []
    kv_bufs = None
    x_in, f_in = x.reshape(T, D_MODEL), None
    for l in range(DEPTH):
        w = weights[l]
        if l == 0:
            ln_g, ln_b, prev_mod = prm["ln_in_g"], prm["ln_in_b"], None
        else:
            ln_g, ln_b, prev_mod = prm["ln2_g"][l - 1], prm["ln2_b"][l - 1], mods[l - 1]
        p = _mixin_call(l, x_in, f_in, prev_mod, ln_g, ln_b, mods[l], w, n_seq, seq_len, tm, prompt, kv_bufs)
        o_a, s_new = _hgrn_call(l, p, prm["hgrn_lower_bounds"], prm["hgrn_norm_g"][l], s0_all[l],
                                n_seq, seq_len)
        if prompt:
            kv_bufs = (p["k"], p["v"], p["lf"])
            o_b = _fox_call(p, n_seq, seq_len, min(512, seq_len))
        else:
            o_b = _decode_call(l, p, cache_kt, cache_vt, cache_lft, page_table, n_seq, seq_len)
            ks.append(p["k"])
            vs.append(p["v"])
            lfs.append(p["lf"])
        x1, h2, cw, route = _merge_call(o_a, o_b, p, mods[l], w, prm["ln1_g"][l], prm["ln1_b"][l],
                                        w_router_t, prm["b_router"], seq_len, min(512, T), win)
        f = _moe_call(h2, cw, route, w, win)
        x_in, f_in = x1, f
        states.append(s_new)
    y = _final_call(x_in, f_in, mods[DEPTH - 1], prm["ln2_g"][DEPTH - 1], prm["ln2_b"][DEPTH - 1], seq_len, tm)
    y = y.reshape(n_seq, seq_len, D_MODEL)
    if prompt:
        kb, vb, lfb = kv_bufs
        k_out = jnp.transpose(kb.reshape(n_seq, DEPTH, FOX_HEADS, FOX_HEAD_DIM, seq_len), (0, 4, 1, 2, 3))
        v_out = jnp.transpose(vb.reshape(n_seq, DEPTH, FOX_HEADS, FOX_HEAD_DIM, seq_len), (0, 4, 1, 2, 3))
        lf_out = jnp.transpose(lfb, (0, 3, 1, 2))
    else:
        k_out = jnp.stack(ks, axis=1).reshape(n_seq, seq_len, DEPTH, FOX_HEADS, FOX_HEAD_DIM)
        v_out = jnp.stack(vs, axis=1).reshape(n_seq, seq_len, DEPTH, FOX_HEADS, FOX_HEAD_DIM)
        lf_out = jnp.stack(lfs, axis=1).reshape(n_seq, seq_len, DEPTH, FOX_HEADS)
    return y, k_out, v_out, lf_out, jnp.stack(states, axis=0)


def kernel(x_prompt, x_sample, c_prompt, c_sample, cache_k, cache_v, cache_logf, state_hgrn, page_table,
           ln_in_g, ln_in_b, w_ada, b_ada, w_in, b_fox_f, hgrn_lower_bounds, hgrn_norm_g,
           w_branch_a, w_branch_b, w_out, ln1_g, ln1_b, w_router, b_router,
           w_exp_gate, w_exp_up, w_exp_down, ln2_g, ln2_b):
    prm = dict(ln_in_g=ln_in_g, ln_in_b=ln_in_b, w_in=w_in, b_fox_f=b_fox_f,
               hgrn_lower_bounds=hgrn_lower_bounds, hgrn_norm_g=hgrn_norm_g, w_branch_a=w_branch_a,
               w_branch_b=w_branch_b, w_out=w_out, ln1_g=ln1_g, ln1_b=ln1_b, w_router=w_router,
               b_router=b_router, w_exp_gate=w_exp_gate, w_exp_up=w_exp_up, w_exp_down=w_exp_down,
               ln2_g=ln2_g, ln2_b=ln2_b)
    n_p, n_s = x_prompt.shape[0], x_sample.shape[0]
    dec_seq = x_sample.shape[1]
    mod_all = _ada_call(jnp.concatenate([c_prompt, c_sample], axis=0), w_ada, b_ada)
    mods_p, mods_s = [], []
    for l in range(DEPTH):
        mp = mod_all[l, :n_p].reshape(n_p, N_MOD, D_MODEL).transpose(1, 0, 2)
        mods_p.append(mp[:, :, None, :])
        ms = mod_all[l, n_p:].reshape(n_s, N_MOD, D_MODEL).transpose(1, 0, 2)
        mods_s.append(jnp.repeat(ms, dec_seq, axis=1)[:, None, :, :])
    weights = [_layer_weights(prm, l) for l in range(DEPTH)]

    y_p, k_p, v_p, lf_p, hg_p = _trunk(x_prompt, mods_p, None, None, prm, weights)
    y_s, k_s, v_s, lf_s, hg_s = _trunk(x_sample, mods_s, state_hgrn,
                                       (cache_k, cache_v, cache_logf, page_table), prm, weights)
    return (y_p, y_s, k_p, v_p, lf_p, hg_p.astype(x_prompt.dtype),
            k_s, v_s, lf_s, hg_s.astype(state_hgrn.dtype))
```

```python
import functools

import jax
import jax.numpy as jnp
import numpy as np
from jax import lax
from jax.experimental import pallas as pl
from jax.experimental.pallas import tpu as pltpu

F32 = jnp.float32
BF16 = jnp.bfloat16

D_MODEL = 1024
DEPTH = 4
PAGE_SIZE = 128
HG_WIDTH = 512
HG_HEADS = 4
HG_DK = 128
HG_CHUNK = 64
HG_SUB = 16
LB_FLOOR = 1e-30
EXP_CLAMP = 80.0
FOX_HEADS = 8
FOX_HEAD_DIM = 64
FOX_WIDTH = 512
MASK_VALUE = -1e30
N_EXPERTS = 16
N_GROUPS = 4
EXPERTS_PER_GROUP = 4
D_EXPERT = 512
N_MOD = 6
DEEPNORM_ALPHA = (2 * DEPTH) ** 0.25
LN_EPS = 1e-5
RMS_EPS = 1e-6
LANES = 128
VMEM_LIMIT = 56 * 1024 * 1024

NN_DIMS = (((1,), (0,)), ((), ()))
NT_DIMS = (((1,), (1,)), ((), ()))
TN_DIMS = (((0,), (0,)), ((), ()))


def _cparams(sem):
    return pltpu.CompilerParams(dimension_semantics=sem, vmem_limit_bytes=VMEM_LIMIT)


def _dot(a, b, dims=NN_DIMS):
    return lax.dot_general(a, b, dims, preferred_element_type=F32)


def _split3(x):
    x1 = x.astype(BF16).astype(F32)
    r1 = x - x1
    x2 = r1.astype(BF16).astype(F32)
    x3 = (r1 - x2).astype(BF16).astype(F32)
    return (x1, x2, x3)


def _dot_sel_lhs(sel, b, dims=NN_DIMS):
    b1, b2, b3 = _split3(b)
    return _dot(sel, b3, dims) + _dot(sel, b2, dims) + _dot(sel, b1, dims)


def _dot_sel_rhs(a, sel, dims=NN_DIMS):
    a1, a2, a3 = _split3(a)
    return _dot(a3, sel, dims) + _dot(a2, sel, dims) + _dot(a1, sel, dims)


def _dot_f32(a, b, dims=NN_DIMS):
    a1, a2, a3 = _split3(a)
    b1, b2, b3 = _split3(b)
    small = _dot(a2, b2, dims) + _dot(a1, b3, dims) + _dot(a3, b1, dims)
    mid = _dot(a1, b2, dims) + _dot(a2, b1, dims)
    return small + mid + _dot(a1, b1, dims)


def _layer_norm(x, g, b):
    xc = x - jnp.mean(x, axis=-1, keepdims=True)
    var = jnp.mean(xc * xc, axis=-1, keepdims=True)
    return xc * lax.rsqrt(var + LN_EPS) * g + b


def _silu(x):
    return x * jax.nn.sigmoid(x)


def _log_sigmoid(x):
    return jnp.minimum(x, 0.0) - jnp.log1p(jnp.exp(-jnp.abs(x)))


def _const_spec(shape):
    nd = len(shape)
    return pl.BlockSpec(shape, lambda *_: (0,) * nd)


def _tri(n, lower):
    r = lax.broadcasted_iota(jnp.int32, (n, n), 0)
    c = lax.broadcasted_iota(jnp.int32, (n, n), 1)
    return jnp.where((r >= c) if lower else (r <= c), 1.0, 0.0).astype(F32)


def _group_specs(tm):
    return [pl.BlockSpec((None, tm, D_MODEL), lambda i, g=g: (g, i, 0)) for g in range(N_GROUPS)]


def _group_sum(refs):
    return (refs[0][...] + refs[1][...]) + (refs[2][...] + refs[3][...])


def _mod_spec(mod, k, tm, tiles_per_seq):
    if mod.shape[2] == 1:
        return pl.BlockSpec((1, 1, 1, D_MODEL), lambda i: (k, i // tiles_per_seq, 0, 0))
    return pl.BlockSpec((1, 1, tm, D_MODEL), lambda i: (k, i, 0, 0))


def _ada_kernel(c_ref, w_ref, b_ref, o_ref):
    s = _silu(c_ref[...]).astype(BF16)
    o_ref[0] = _dot(s, w_ref[0].astype(BF16)) + b_ref[0]


def _ada_call(c_all, w_ada, b_ada):
    n = c_all.shape[0]
    width = N_MOD * D_MODEL
    tn = 1536
    return pl.pallas_call(
        _ada_kernel,
        grid=(DEPTH, width // tn),
        in_specs=[
            pl.BlockSpec((n, D_MODEL), lambda l, j: (0, 0)),
            pl.BlockSpec((1, D_MODEL, tn), lambda l, j: (l, 0, j)),
            pl.BlockSpec((1, 1, tn), lambda l, j: (l, 0, j)),
        ],
        out_specs=pl.BlockSpec((1, n, tn), lambda l, j: (l, 0, j)),
        out_shape=jax.ShapeDtypeStruct((DEPTH, n, width), F32),
        compiler_params=_cparams(("arbitrary", "arbitrary")),
        name="ada_mod",
    )(c_all, w_ada, b_ada.reshape(DEPTH, 1, width))


def _mixin_kernel(has_prev, prompt, n_alias, tiles_per_seq, *refs):
    it = iter(refs)
    xin_ref = next(it)
    if has_prev:
        fin_refs = [next(it) for _ in range(N_GROUPS)]
        g2_ref = next(it)
    lng_ref, lnb_ref, sh_ref, sc_ref = next(it), next(it), next(it), next(it)
    wh_ref, wq_ref, wk_ref, wv_ref, wff_ref, wg_ref = (next(it), next(it), next(it), next(it),
                                                       next(it), next(it))
    bff_ref, bfft_ref = next(it), next(it)
    if prompt:
        tril_ref, triu_ref, plq_ref, plk_ref = next(it), next(it), next(it), next(it)
    for _ in range(n_alias):
        next(it)
    x_ref, hq_ref, hf_ref, hi_ref, hg_ref = next(it), next(it), next(it), next(it), next(it)
    fq_ref, ga_ref, gb_ref = next(it), next(it), next(it)
    k_ref, v_ref, lf_ref = next(it), next(it), next(it)
    if prompt:
        ka_ref, va_ref, carry_c, carry_r = next(it), next(it), next(it), next(it)

    x = xin_ref[...]
    if has_prev:
        x = DEEPNORM_ALPHA * x + g2_ref[0, 0] * _group_sum(fin_refs)
    x = _layer_norm(x, lng_ref[...], lnb_ref[...])
    x_ref[...] = x
    h = (x * (1.0 + sc_ref[0, 0]) + sh_ref[0, 0]).astype(BF16)
    h32 = h.astype(F32)

    ph = _dot(h, wh_ref[...])
    hq_ref[...] = ph[:, 0 * HG_WIDTH:1 * HG_WIDTH]
    hf_ref[...] = ph[:, 1 * HG_WIDTH:2 * HG_WIDTH]
    hi_ref[...] = ph[:, 2 * HG_WIDTH:3 * HG_WIDTH]
    hg_ref[...] = ph[:, 3 * HG_WIDTH:4 * HG_WIDTH]
    q = _dot(h, wq_ref[...]) * (FOX_HEAD_DIM ** -0.5)
    pg = _dot(h, wg_ref[...])
    ga_ref[...] = pg[:, 0:D_MODEL]
    gb_ref[...] = pg[:, D_MODEL:2 * D_MODEL]

    if not prompt:
        fq_ref[...] = q
        k_ref[...] = _dot(h, wk_ref[...])
        v_ref[...] = _dot(h, wv_ref[...])
        lf_ref[...] = _log_sigmoid(_dot(h32, wff_ref[...]) + bff_ref[...])
    else:
        kvf = _dot(h, wk_ref[...])
        kt = kvf[:, 0:FOX_WIDTH].T
        vt = kvf[:, FOX_WIDTH:2 * FOX_WIDTH].T
        k_ref[0, 0] = kt
        v_ref[0, 0] = vt
        ff = kvf[:, 2 * FOX_WIDTH:2 * FOX_WIDTH + LANES]
        lf = _log_sigmoid(ff[:, 0:FOX_HEADS] + bff_ref[...])
        lft = _log_sigmoid(ff.T[0:FOX_HEADS, :] + bfft_ref[...])
        lf_ref[0, 0] = lft

        @pl.when(pl.program_id(0) % tiles_per_seq == 0)
        def _():
            carry_c[...] = jnp.zeros_like(carry_c)
            carry_r[...] = jnp.zeros_like(carry_r)

        fcol = _dot_sel_lhs(tril_ref[...], lf) + carry_c[...]
        frow = _dot_sel_rhs(lft, triu_ref[...]) + carry_r[...]
        tm = lf.shape[0]
        carry_c[...] = fcol[tm - 1:tm, :]
        carry_r[...] = frow[:, tm - 1:tm]

        low = lax.broadcasted_iota(jnp.int32, (1, LANES), 1) < FOX_HEAD_DIM
        q_aug = _dot(jnp.concatenate(list(_split3(fcol)) + [jnp.ones((tm, FOX_HEADS), F32)], axis=1),
                     plq_ref[...])
        k_aug = _dot(plk_ref[...],
                     jnp.concatenate([-r for r in _split3(frow)] + [jnp.ones((FOX_HEADS, tm), F32)], axis=0))
        ones = jnp.ones((FOX_HEAD_DIM, tm), F32)
        for hd in range(FOX_HEADS):
            feat = slice(hd * FOX_HEAD_DIM, (hd + 1) * FOX_HEAD_DIM)
            blk = slice(hd * LANES, (hd + 1) * LANES)
            src = q[:, (hd // 2) * LANES:(hd // 2 + 1) * LANES]
            if hd % 2 == 0:
                fq_ref[:, blk] = jnp.where(low, src, q_aug[:, blk]).astype(BF16)
                ka_ref[0, blk, :] = jnp.concatenate([kt[feat, :], k_aug[feat, :]], axis=0).astype(BF16)
                va_ref[0, blk, :] = jnp.concatenate([vt[feat, :], ones], axis=0).astype(BF16)
            else:
                fq_ref[:, blk] = jnp.where(low, q_aug[:, blk], src).astype(BF16)
                ka_ref[0, blk, :] = jnp.concatenate([k_aug[feat, :], kt[feat, :]], axis=0).astype(BF16)
                va_ref[0, blk, :] = jnp.concatenate([ones, vt[feat, :]], axis=0).astype(BF16)


def _bias_placements():
    plq = np.zeros((4 * FOX_HEADS, FOX_HEADS * LANES), np.float32)
    plk = np.zeros((FOX_HEADS * FOX_HEAD_DIM, 4 * FOX_HEADS), np.float32)
    for hd in range(FOX_HEADS):
        spare = hd * LANES + (FOX_HEAD_DIM if hd % 2 == 0 else 0)
        for j in range(3):
            plq[j * FOX_HEADS + hd, spare + j] = 1.0
            plq[3 * FOX_HEADS + hd, spare + 3 + j] = 1.0
            plk[hd * FOX_HEAD_DIM + j, 3 * FOX_HEADS + hd] = 1.0
            plk[hd * FOX_HEAD_DIM + 3 + j, j * FOX_HEADS + hd] = 1.0
    return jnp.asarray(plq), jnp.asarray(plk)


def _mixin_call(layer, x_in, f_in, prev_mod, ln_g, ln_b, mod, w, n_seq, seq_len, tm, prompt, kv_bufs):
    T = x_in.shape[0]
    tiles_per_seq = max(seq_len // tm, 1)
    tok = lambda width: pl.BlockSpec((tm, width), lambda i: (i, 0))
    has_prev = f_in is not None

    args, specs = [x_in], [tok(D_MODEL)]
    if has_prev:
        args += [f_in] * N_GROUPS + [prev_mod]
        specs += _group_specs(tm) + [_mod_spec(prev_mod, 5, tm, tiles_per_seq)]
    args += [ln_g.reshape(1, D_MODEL), ln_b.reshape(1, D_MODEL), mod, mod]
    specs += [_const_spec((1, D_MODEL)), _const_spec((1, D_MODEL)),
              _mod_spec(mod, 0, tm, tiles_per_seq), _mod_spec(mod, 1, tm, tiles_per_seq)]
    if prompt:
        wnames = ("wh", "wq", "wkvf", "wff", "wff", "wg", "bff", "bfft")
    else:
        wnames = ("wh", "wq", "wk", "wv", "wff", "wg", "bff", "bfft")
    args += [w[n] for n in wnames]
    specs += [_const_spec(w[n].shape) for n in wnames]
    if prompt:
        plq, plk = _bias_placements()
        args += [_tri(tm, True), _tri(tm, False), plq, plk]
        specs += [_const_spec((tm, tm)), _const_spec((tm, tm)), _const_spec(plq.shape), _const_spec(plk.shape)]

    names = ["x", "hq", "hf", "hi", "hg", "fq", "ga", "gb", "k", "v", "lf"]
    widths = [D_MODEL, HG_WIDTH, HG_WIDTH, HG_WIDTH, HG_WIDTH,
              FOX_HEADS * LANES if prompt else FOX_WIDTH, D_MODEL, D_MODEL]
    dtypes = [F32] * 5 + [BF16 if prompt else F32] + [F32] * 2
    out_shape = [jax.ShapeDtypeStruct((T, wd), dt) for wd, dt in zip(widths, dtypes)]
    out_specs = [tok(wd) for wd in widths]
    scratch = []
    aliases = {}
    if prompt:
        fm = lambda rows: pl.BlockSpec((1, 1, rows, tm),
                                       lambda i: (i // tiles_per_seq, layer, 0, i % tiles_per_seq))
        out_shape += [jax.ShapeDtypeStruct((n_seq, DEPTH, FOX_WIDTH, seq_len), F32),
                      jax.ShapeDtypeStruct((n_seq, DEPTH, FOX_WIDTH, seq_len), F32),
                      jax.ShapeDtypeStruct((n_seq, DEPTH, FOX_HEADS, seq_len), F32)]
        out_specs += [fm(FOX_WIDTH), fm(FOX_WIDTH), fm(FOX_HEADS)]
        if kv_bufs is not None:
            for j, buf in enumerate(kv_bufs):
                aliases[len(args)] = 8 + j
                args.append(buf)
                specs.append(pl.BlockSpec(memory_space=pl.ANY))
        names += ["ka", "va"]
        aug = jax.ShapeDtypeStruct((n_seq, FOX_HEADS * LANES, seq_len), BF16)
        aug_spec = pl.BlockSpec((1, FOX_HEADS * LANES, tm),
                                lambda i: (i // tiles_per_seq, 0, i % tiles_per_seq))
        out_shape += [aug, aug]
        out_specs += [aug_spec, aug_spec]
        scratch = [pltpu.VMEM((1, FOX_HEADS), F32), pltpu.VMEM((FOX_HEADS, 1), F32)]
    else:
        out_shape += [jax.ShapeDtypeStruct((T, FOX_WIDTH), F32), jax.ShapeDtypeStruct((T, FOX_WIDTH), F32),
                      jax.ShapeDtypeStruct((T, FOX_HEADS), F32)]
        out_specs += [tok(FOX_WIDTH), tok(FOX_WIDTH), tok(FOX_HEADS)]

    outs = pl.pallas_call(
        functools.partial(_mixin_kernel, has_prev, prompt, len(aliases), tiles_per_seq),
        grid=(T // tm,),
        in_specs=specs,
        out_specs=out_specs,
        out_shape=out_shape,
        scratch_shapes=scratch,
        input_output_aliases=aliases,
        compiler_params=_cparams(("arbitrary",)),
        name="mixer_in",
    )(*args)
    return dict(zip(names, outs))


def _cumsum_rows(x):
    n = x.shape[0]
    row = lax.broadcasted_iota(jnp.int32, x.shape, 0)
    shift = 1
    while shift < n:
        x = x + jnp.where(row >= shift, pltpu.roll(x, shift, 0), 0.0)
        shift *= 2
    return x


def _hgrn_kernel(layer, chunk, sub, n_chunks, hq_ref, hf_ref, hi_ref, hg_ref, lb_ref, ng_ref, s0_ref,
                 o_ref, sout_ref, st_ref):
    t = pl.program_id(1)
    mm = BF16 if chunk >= 16 else F32

    @pl.when(t == 0)
    def _():
        for hd in range(HG_HEADS):
            st_ref[hd] = s0_ref[0, hd].T

    lb_all = lb_ref[...]
    e = jnp.exp(lb_all - jnp.max(lb_all, axis=0, keepdims=True))
    p = e / jnp.sum(e, axis=0, keepdims=True)
    lb = jnp.zeros((1, HG_WIDTH), F32)
    for j in range(1, layer + 1):
        lb = lb + p[j:j + 1, :]
    log_lb = jnp.log(jnp.maximum(lb, LB_FLOOR))
    log1m_lb = jnp.log1p(-lb)
    n_sub = chunk // sub
    causal = (lax.broadcasted_iota(jnp.int32, (chunk, chunk), 1)
              <= lax.broadcasted_iota(jnp.int32, (chunk, chunk), 0))

    def chunk_body(ci, carry):
        r0 = pl.multiple_of(ci * chunk, chunk)
        for hd in range(HG_HEADS):
            ls = slice(hd * HG_DK, (hd + 1) * HG_DK)
            z = hf_ref[pl.ds(r0, chunk), ls]
            q = _silu(hq_ref[pl.ds(r0, chunk), ls])
            v = hi_ref[pl.ds(r0, chunk), ls].astype(mm)
            a = log_lb[:, ls]
            b = log1m_lb[:, ls] + _log_sigmoid(z)
            log_f = jnp.maximum(a, b) + jnp.log1p(jnp.exp(-jnp.abs(a - b)))
            k = (1.0 - lb[:, ls]) * jax.nn.sigmoid(-z)
            cum = _cumsum_rows(log_f)
            blocks = []
            for bi in range(n_sub):
                rows = slice(bi * sub, (bi + 1) * sub)
                base = jnp.zeros((1, HG_DK), F32) if bi == 0 else cum[bi * sub - 1:bi * sub, :]
                k_i = (k * jnp.exp(jnp.minimum(base - cum, EXP_CLAMP))).astype(mm)
                q_i = (q[rows, :] * jnp.exp(cum[rows, :] - base)).astype(mm)
                blocks.append(_dot(q_i, k_i, NT_DIMS))
            attn = blocks[0] if n_sub == 1 else jnp.concatenate(blocks, axis=0)
            attn = jnp.where(causal, attn, 0.0).astype(mm)
            s_t = st_ref[hd]
            o = _dot(attn, v) + _dot((q * jnp.exp(cum)).astype(mm), s_t.astype(mm), NT_DIMS)
            cum_end = cum[chunk - 1:chunk, :]
            k_end = (k * jnp.exp(cum_end - cum)).astype(mm)
            st_ref[hd] = jnp.exp(cum_end) * s_t + _dot(v, k_end, TN_DIMS)
            o = o * lax.rsqrt(jnp.mean(o * o, axis=-1, keepdims=True) + RMS_EPS)
            o_ref[pl.ds(r0, chunk), ls] = o * ng_ref[:, ls] * _silu(hg_ref[pl.ds(r0, chunk), ls])
        return carry

    lax.fori_loop(0, n_chunks, chunk_body, 0, unroll=min(8, n_chunks))

    @pl.when(t == pl.num_programs(1) - 1)
    def _():
        for hd in range(HG_HEADS):
            sout_ref[0, hd] = st_ref[hd].T


def _hgrn_call(layer, p, lower_bounds, norm_g, s0, n_seq, seq_len):
    chunk = HG_CHUNK if seq_len % HG_CHUNK == 0 else seq_len
    sub = min(HG_SUB, chunk)
    tb = min(seq_len, 512)
    nt = seq_len // tb
    tok = pl.BlockSpec((tb, HG_WIDTH), lambda b, t: (b * nt + t, 0))
    st_spec = pl.BlockSpec((1, HG_HEADS, HG_DK, HG_DK), lambda b, t: (b, 0, 0, 0))
    return pl.pallas_call(
        functools.partial(_hgrn_kernel, layer, chunk, sub, tb // chunk),
        grid=(n_seq, nt),
        in_specs=[tok, tok, tok, tok,
                  pl.BlockSpec((DEPTH, HG_WIDTH), lambda b, t: (0, 0)),
                  pl.BlockSpec((1, HG_WIDTH), lambda b, t: (0, 0)),
                  st_spec],
        out_specs=[tok, st_spec],
        out_shape=[jax.ShapeDtypeStruct((n_seq * seq_len, HG_WIDTH), F32),
                   jax.ShapeDtypeStruct((n_seq, HG_HEADS, HG_DK, HG_DK), F32)],
        scratch_shapes=[pltpu.VMEM((HG_HEADS, HG_DK, HG_DK), F32)],
        compiler_params=_cparams(("arbitrary", "arbitrary")),
        name="hgrn",
    )(p["hq"], p["hf"], p["hi"], p["hg"], lower_bounds, norm_g.reshape(1, HG_WIDTH), s0)


def _fox_kernel(tq, tk, qi_ref, ki_ref, q_ref, k_ref, v_ref, o_ref, m_ref, acc_ref):
    qi = qi_ref[pl.program_id(1)]
    ki = ki_ref[pl.program_id(1)]

    @pl.when(ki == 0)
    def _():
        m_ref[...] = jnp.full_like(m_ref, MASK_VALUE)
        acc_ref[...] = jnp.zeros_like(acc_ref)

    def absorb(diagonal):
        if diagonal:
            visible = (lax.broadcasted_iota(jnp.int32, (tq, tk), 1)
                       <= lax.broadcasted_iota(jnp.int32, (tq, tk), 0))
        for hd in range(FOX_HEADS):
            blk = slice(hd * LANES, (hd + 1) * LANES)
            s = _dot(q_ref[:, blk], k_ref[0, blk, :])
            if diagonal:
                s = jnp.where(visible, s, MASK_VALUE)
            cols = [s[:, c * LANES:(c + 1) * LANES] for c in range(tk // LANES)]
            cmax = cols[0]
            for c in cols[1:]:
                cmax = jnp.maximum(cmax, c)
            m_old = m_ref[hd]
            m_new = jnp.maximum(m_old, jnp.broadcast_to(jnp.max(cmax, axis=-1, keepdims=True), (tq, LANES)))
            pe = jnp.concatenate([jnp.exp(c - m_new) for c in cols], axis=1).astype(BF16)
            m_ref[hd] = m_new
            acc_ref[:, blk] = jnp.exp(m_old - m_new) * acc_ref[:, blk] + _dot(pe, v_ref[0, blk, :], NT_DIMS)

    @pl.when(ki < qi)
    def _():
        absorb(False)

    @pl.when(ki == qi)
    def _():
        absorb(True)
        low = lax.broadcasted_iota(jnp.int32, (1, LANES), 1) < FOX_HEAD_DIM
        for pr in range(FOX_HEADS // 2):
            even = acc_ref[:, 2 * pr * LANES:(2 * pr + 1) * LANES]
            odd = acc_ref[:, (2 * pr + 1) * LANES:(2 * pr + 2) * LANES]
            o_ref[:, pr * LANES:(pr + 1) * LANES] = jnp.where(
                low, even / pltpu.roll(even, FOX_HEAD_DIM, 1), odd / pltpu.roll(odd, FOX_HEAD_DIM, 1))


def _fox_call(p, n_seq, seq_len, tq):
    tk = tq
    nq = seq_len // tq
    T = n_seq * seq_len
    width = FOX_HEADS * LANES
    pairs = [(qi, ki) for qi in range(nq) for ki in range(qi + 1)]
    qi_tab = jnp.asarray([pr[0] for pr in pairs], jnp.int32)
    ki_tab = jnp.asarray([pr[1] for pr in pairs], jnp.int32)
    kspec = pl.BlockSpec((1, width, tk), lambda b, s, qt, kt: (b, 0, kt[s]))
    grid_spec = pltpu.PrefetchScalarGridSpec(
        num_scalar_prefetch=2,
        grid=(n_seq, len(pairs)),
        in_specs=[pl.BlockSpec((tq, width), lambda b, s, qt, kt: (b * nq + qt[s], 0)), kspec, kspec],
        out_specs=pl.BlockSpec((tq, FOX_WIDTH), lambda b, s, qt, kt: (b * nq + qt[s], 0)),
        scratch_shapes=[pltpu.VMEM((FOX_HEADS, tq, LANES), F32), pltpu.VMEM((tq, width), F32)],
    )
    return pl.pallas_call(
        functools.partial(_fox_kernel, tq, tk),
        grid_spec=grid_spec,
        out_shape=jax.ShapeDtypeStruct((T, FOX_WIDTH), F32),
        compiler_params=_cparams(("arbitrary", "arbitrary")),
        name="fox_prompt",
    )(qi_tab, ki_tab, p["fq"], p["ka"], p["va"])


PAGES_PER_STEP = 16


def _decode_kernel(n_new, pt_ref, q_ref, kn_ref, vn_ref, lfn_ref, *refs):
    npg = PAGES_PER_STEP
    k_refs = refs[0:npg]
    v_refs = refs[npg:2 * npg]
    lf_refs = refs[2 * npg:3 * npg]
    o_ref, m_ref, l_ref, acc_ref, carry_ref = refs[3 * npg:]
    j = pl.program_id(1)
    rows = FOX_HEADS * n_new

    @pl.when(j == 0)
    def _():
        m_ref[...] = jnp.full_like(m_ref, MASK_VALUE)
        l_ref[...] = jnp.zeros_like(l_ref)
        acc_ref[...] = jnp.zeros_like(acc_ref)
        carry_ref[...] = jnp.zeros_like(carry_ref)

    q = q_ref[...]
    rr = lax.broadcasted_iota(jnp.int32, (rows, FOX_WIDTH), 0)
    cc = lax.broadcasted_iota(jnp.int32, (rows, FOX_WIDTH), 1)
    q_rep = jnp.concatenate([q] * FOX_HEADS, axis=0)
    qbd = jnp.where((rr // n_new) == (cc // FOX_HEAD_DIM), q_rep, 0.0)
    qbd_bf = qbd.astype(BF16)
    ehe = jnp.where(lax.broadcasted_iota(jnp.int32, (rows, FOX_HEADS), 0) // n_new
                    == lax.broadcasted_iota(jnp.int32, (rows, FOX_HEADS), 1), 1.0, 0.0)
    triu = _tri(PAGE_SIZE, False)

    def absorb(s, pv_fn):
        m_old = m_ref[...]
        m_new = jnp.maximum(m_old, jnp.max(s, axis=-1, keepdims=True))
        alpha = jnp.exp(m_old - m_new)
        pe = jnp.exp(s - m_new)
        l_ref[...] = alpha * l_ref[...] + jnp.sum(pe, axis=-1, keepdims=True)
        m_ref[...] = m_new
        acc_ref[...] = alpha * acc_ref[...] + pv_fn(pe)

    lf_all = jnp.concatenate([lf_refs[pg][...] for pg in range(npg)], axis=0)
    within = _dot_sel_rhs(lf_all, triu)
    nr = npg * FOX_HEADS
    ri = lax.broadcasted_iota(jnp.int32, (nr, nr), 0)
    ci = lax.broadcasted_iota(jnp.int32, (nr, nr), 1)
    earlier = jnp.where(jnp.logical_and(ri % FOX_HEADS == ci % FOX_HEADS, ci < ri), 1.0, 0.0)
    totals = jnp.broadcast_to(within[:, PAGE_SIZE - 1:PAGE_SIZE], (nr, PAGE_SIZE))
    carry_rep = jnp.concatenate([jnp.broadcast_to(carry_ref[...], (FOX_HEADS, PAGE_SIZE))] * npg, axis=0)
    f_all = within + _dot_sel_lhs(earlier, totals) + carry_rep
    carry_ref[...] = f_all[nr - FOX_HEADS:nr, PAGE_SIZE - 1:PAGE_SIZE]
    bias = jnp.concatenate(
        [jnp.concatenate([jnp.broadcast_to(f_all[pg * FOX_HEADS + hd:pg * FOX_HEADS + hd + 1, :],
                                           (n_new, PAGE_SIZE)) for hd in range(FOX_HEADS)], axis=0)
         for pg in range(npg)], axis=1)
    kt = jnp.concatenate([k_refs[pg][...].astype(BF16) for pg in range(npg)], axis=1)
    vt = jnp.concatenate([v_refs[pg][...].astype(BF16) for pg in range(npg)], axis=1)
    absorb(_dot(qbd_bf, kt) - bias, lambda pe: _dot(pe.astype(BF16), vt, NT_DIMS))

    @pl.when(j == pl.num_programs(1) - 1)
    def _():
        kn = kn_ref[...]
        vn = vn_ref[...]
        eye = (lax.broadcasted_iota(jnp.int32, (FOX_HEADS, FOX_HEADS), 0)
               == lax.broadcasted_iota(jnp.int32, (FOX_HEADS, FOX_HEADS), 1))
        carry_row = jnp.sum(jnp.where(eye, carry_ref[...], 0.0), axis=0, keepdims=True)
        f_new = _dot_sel_lhs(_tri(n_new, True), lfn_ref[...]) + carry_row
        s = _dot(qbd, kn, NT_DIMS) - _dot_sel_lhs(ehe, f_new, NT_DIMS)
        key_i = lax.broadcasted_iota(jnp.int32, (rows, n_new), 1)
        qry_i = lax.broadcasted_iota(jnp.int32, (rows, n_new), 0) % n_new
        s = jnp.where(key_i <= qry_i, s, MASK_VALUE)
        absorb(s, lambda pe: _dot(pe, vn))
        out = acc_ref[...] / l_ref[...]
        lane_head = lax.broadcasted_iota(jnp.int32, (n_new, FOX_WIDTH), 1) // FOX_HEAD_DIM
        res = jnp.zeros((n_new, FOX_WIDTH), F32)
        for hd in range(FOX_HEADS):
            res = res + jnp.where(lane_head == hd, out[hd * n_new:(hd + 1) * n_new, :], 0.0)
        o_ref[...] = res


def _decode_call(layer, p, cache_kt, cache_vt, cache_lft, page_table, n_seq, n_new):
    n_pages = page_table.shape[1]
    npg = PAGES_PER_STEP
    pt_flat = page_table.reshape(-1).astype(jnp.int32)

    def page_map(pg):
        return lambda n, j, pt: (pt[n * n_pages + j * npg + pg], layer, 0, 0)

    new_w = pl.BlockSpec((n_new, FOX_WIDTH), lambda n, j, pt: (n, 0))
    in_specs = [new_w, new_w, new_w, pl.BlockSpec((n_new, FOX_HEADS), lambda n, j, pt: (n, 0))]
    in_specs += [pl.BlockSpec((None, None, FOX_WIDTH, PAGE_SIZE), page_map(pg)) for pg in range(npg)]
    in_specs += [pl.BlockSpec((None, None, FOX_WIDTH, PAGE_SIZE), page_map(pg)) for pg in range(npg)]
    in_specs += [pl.BlockSpec((None, None, FOX_HEADS, PAGE_SIZE), page_map(pg)) for pg in range(npg)]
    rows = FOX_HEADS * n_new
    grid_spec = pltpu.PrefetchScalarGridSpec(
        num_scalar_prefetch=1,
        grid=(n_seq, n_pages // npg),
        in_specs=in_specs,
        out_specs=new_w,
        scratch_shapes=[pltpu.VMEM((rows, 1), F32), pltpu.VMEM((rows, 1), F32),
                        pltpu.VMEM((rows, FOX_WIDTH), F32), pltpu.VMEM((FOX_HEADS, 1), F32)],
    )
    return pl.pallas_call(
        functools.partial(_decode_kernel, n_new),
        grid_spec=grid_spec,
        out_shape=jax.ShapeDtypeStruct((n_seq * n_new, FOX_WIDTH), F32),
        compiler_params=_cparams(("arbitrary", "arbitrary")),
        name="fox_decode",
    )(pt_flat, p["fq"], p["k"], p["v"], p["lf"],
      *([cache_kt] * npg), *([cache_vt] * npg), *([cache_lft] * npg))


def _merge_kernel(tiles_per_win, part, oa_ref, ob_ref, ga_ref, gb_ref, x_ref, g1_ref, sh_ref, sc_ref,
                  wa_ref, wb_ref, wo_ref, lng_ref, lnb_ref, wrt_ref, br_ref, triu_ref,
                  x1_ref, h2_ref, cw_ref, route_ref, carry_ref):
    @pl.when(pl.program_id(0) % tiles_per_win == 0)
    def _():
        carry_ref[...] = jnp.zeros_like(carry_ref)

    for pi in range(x_ref.shape[0] // part):
        _merge_part(slice(pi * part, (pi + 1) * part), oa_ref, ob_ref, ga_ref, gb_ref, x_ref, g1_ref,
                    sh_ref, sc_ref, wa_ref, wb_ref, wo_ref, lng_ref, lnb_ref, wrt_ref, br_ref, triu_ref,
                    x1_ref, h2_ref, cw_ref, route_ref, carry_ref)


def _merge_part(rs, oa_ref, ob_ref, ga_ref, gb_ref, x_ref, g1_ref, sh_ref, sc_ref,
                wa_ref, wb_ref, wo_ref, lng_ref, lnb_ref, wrt_ref, br_ref, triu_ref,
                x1_ref, h2_ref, cw_ref, route_ref, carry_ref):
    def mod_rows(ref):
        return ref[0, 0] if ref.shape[2] == 1 else ref[0, 0, rs, :]

    ya = _dot(oa_ref[rs, :].astype(BF16), wa_ref[...])
    yb = _dot(ob_ref[rs, :].astype(BF16), wb_ref[...])
    merged = jax.nn.sigmoid(ga_ref[rs, :]) * ya + jax.nn.sigmoid(gb_ref[rs, :]) * yb
    m = _dot(merged.astype(BF16), wo_ref[...])
    x1 = _layer_norm(DEEPNORM_ALPHA * x_ref[rs, :] + mod_rows(g1_ref) * m, lng_ref[...], lnb_ref[...])
    x1_ref[rs, :] = x1
    h2 = x1 * (1.0 + mod_rows(sc_ref)) + mod_rows(sh_ref)
    h2_ref[rs, :] = h2.astype(BF16)

    tm = h2.shape[0]
    hp = [t.astype(BF16) for t in _split3(h2)]
    wr = [wrt_ref[j] for j in range(3)]
    small = _dot(hp[1], wr[1]) + _dot(hp[0], wr[2]) + _dot(hp[2], wr[0])
    mid = _dot(hp[0], wr[1]) + _dot(hp[1], wr[0])
    logits = (small + mid + _dot(hp[0], wr[0])).T[:N_EXPERTS, :] + br_ref[...]
    ex = jnp.exp(logits - jnp.max(logits, axis=0, keepdims=True))
    scores = ex / jnp.sum(ex, axis=0, keepdims=True)
    gs = []
    for g in range(N_GROUPS):
        r = [scores[g * EXPERTS_PER_GROUP + e:g * EXPERTS_PER_GROUP + e + 1, :]
             for e in range(EXPERTS_PER_GROUP)]
        best = r[0] + r[1]
        for a in range(EXPERTS_PER_GROUP):
            for b in range(a + 1, EXPERTS_PER_GROUP):
                best = jnp.maximum(best, r[a] + r[b])
        gs.append(best)
    gmax = jnp.maximum(jnp.maximum(gs[0], gs[1]), jnp.maximum(gs[2], gs[3]))
    gid = jnp.where(gs[0] == gmax, 0, jnp.where(gs[1] == gmax, 1, jnp.where(gs[2] == gmax, 2, 3)))
    erow = lax.broadcasted_iota(jnp.int32, (N_EXPERTS, tm), 0)
    masked = jnp.where(erow // EXPERTS_PER_GROUP == gid, scores, -1.0)
    top1 = jnp.max(masked, axis=0, keepdims=True)
    idx1 = jnp.min(jnp.where(masked == top1, erow, N_EXPERTS), axis=0, keepdims=True)
    masked2 = jnp.where(erow == idx1, -2.0, masked)
    top2 = jnp.max(masked2, axis=0, keepdims=True)
    idx2 = jnp.min(jnp.where(masked2 == top2, erow, N_EXPERTS), axis=0, keepdims=True)
    den = top1 + top2
    cw_ref[:, rs] = jnp.where(erow == idx1, top1 / den, 0.0) + jnp.where(erow == idx2, top2 / den, 0.0)

    grow = lax.broadcasted_iota(jnp.int32, (8, tm), 0)
    member = grow == gid
    incl = _dot(jnp.where(member, 1.0, 0.0), triu_ref[...]) + carry_ref[...]
    carry_ref[...] = incl[:, tm - 1:tm]
    rank = jnp.sum(jnp.where(member, incl, 0.0), axis=0, keepdims=True) - 1.0
    route_ref[:, rs] = jnp.where(grow == 0, gid, jnp.where(grow == 1, rank.astype(jnp.int32), 0))


def _merge_call(o_a, o_b, p, mod, w, ln_g, ln_b, w_router_t, b_router, seq_len, tm, win):
    T = o_a.shape[0]
    part = min(tm, 256)
    tiles_per_seq = max(seq_len // tm, 1)
    tok = lambda width: pl.BlockSpec((tm, width), lambda i: (i, 0))
    return pl.pallas_call(
        functools.partial(_merge_kernel, win // tm, part),
        grid=(T // tm,),
        in_specs=[tok(HG_WIDTH), tok(FOX_WIDTH), tok(D_MODEL), tok(D_MODEL), tok(D_MODEL),
                  _mod_spec(mod, 2, tm, tiles_per_seq), _mod_spec(mod, 3, tm, tiles_per_seq),
                  _mod_spec(mod, 4, tm, tiles_per_seq),
                  _const_spec((HG_WIDTH, D_MODEL)), _const_spec((FOX_WIDTH, D_MODEL)),
                  _const_spec((D_MODEL, D_MODEL)), _const_spec((1, D_MODEL)), _const_spec((1, D_MODEL)),
                  _const_spec((3, D_MODEL, LANES)), _const_spec((N_EXPERTS, 1)), _const_spec((part, part))],
        out_specs=[tok(D_MODEL), tok(D_MODEL),
                   pl.BlockSpec((N_EXPERTS, tm), lambda i: (0, i)),
                   pl.BlockSpec((8, tm), lambda i: (0, i))],
        out_shape=[jax.ShapeDtypeStruct((T, D_MODEL), F32), jax.ShapeDtypeStruct((T, D_MODEL), BF16),
                   jax.ShapeDtypeStruct((N_EXPERTS, T), F32), jax.ShapeDtypeStruct((8, T), jnp.int32)],
        scratch_shapes=[pltpu.VMEM((8, 1), F32)],
        compiler_params=_cparams(("arbitrary",)),
        name="merge",
    )(o_a, o_b, p["ga"], p["gb"], p["x"], mod, mod, mod, w["wa"], w["wb"], w["wo"],
      ln_g.reshape(1, D_MODEL), ln_b.reshape(1, D_MODEL), w_router_t, b_router.reshape(N_EXPERTS, 1),
      _tri(part, False))


MOE_UNIT = 128
MOE_MAX_UNITS = 4


def _moe_kernel(h_ref, cw_ref, route_ref, wg_ref, wu_ref, wd_ref, y_ref):
    g = pl.program_id(0)
    win = h_ref.shape[0]
    y_ref[...] = jnp.zeros_like(y_ref)

    gid = route_ref[0:1, :]
    rank = route_ref[1:2, :]
    member = gid == g
    count = jnp.sum(jnp.where(member, 1, 0))

    def process(base, rows):
        slot = lax.broadcasted_iota(jnp.int32, (rows, win), 0)
        ecol = lax.broadcasted_iota(jnp.int32, (rows, N_EXPERTS), 1)
        perm_f = jnp.where(jnp.logical_and(member, rank - base == slot), 1.0, 0.0)
        perm = perm_f.astype(BF16)
        hs = _dot(perm, h_ref[...]).astype(BF16)
        cwg = _dot_sel_lhs(perm_f, cw_ref[...], NT_DIMS)
        acc = jnp.zeros((rows, D_MODEL), F32)
        for e in range(EXPERTS_PER_GROUP):
            cwe = jnp.sum(jnp.where(ecol == g * EXPERTS_PER_GROUP + e, cwg, 0.0), axis=1, keepdims=True)
            a = _dot(hs, wg_ref[0, e])
            u = _dot(hs, wu_ref[0, e])
            hid = (_silu(a) * u * cwe).astype(BF16)
            acc = acc + _dot(hid, wd_ref[0, e])
        y_ref[...] += _dot(perm, acc.astype(BF16), TN_DIMS)

    big = MOE_UNIT * MOE_MAX_UNITS
    n_big = count // big

    def big_pass(j, carry):
        process(j * big, big)
        return carry

    lax.fori_loop(0, n_big, big_pass, 0)
    rest_units = (count - n_big * big + MOE_UNIT - 1) // MOE_UNIT
    for units in range(1, MOE_MAX_UNITS + 1):
        @pl.when(rest_units == units)
        def _(units=units):
            process(n_big * big, units * MOE_UNIT)


def _moe_call(h2, cw, route, w, win):
    T = h2.shape[0]
    wspec = lambda shp: pl.BlockSpec((1,) + shp, lambda g, i: (g, 0, 0, 0))
    return pl.pallas_call(
        _moe_kernel,
        grid=(N_GROUPS, T // win),
        in_specs=[pl.BlockSpec((win, D_MODEL), lambda g, i: (i, 0)),
                  pl.BlockSpec((N_EXPERTS, win), lambda g, i: (0, i)),
                  pl.BlockSpec((8, win), lambda g, i: (0, i)),
                  wspec((EXPERTS_PER_GROUP, D_MODEL, D_EXPERT)),
                  wspec((EXPERTS_PER_GROUP, D_MODEL, D_EXPERT)),
                  wspec((EXPERTS_PER_GROUP, D_EXPERT, D_MODEL))],
        out_specs=pl.BlockSpec((None, win, D_MODEL), lambda g, i: (g, i, 0)),
        out_shape=jax.ShapeDtypeStruct((N_GROUPS, T, D_MODEL), F32),
        compiler_params=_cparams(("arbitrary", "arbitrary")),
        name="moe",
    )(h2, cw, route, w["eg"], w["eu"], w["ed"])


def _final_kernel(x_ref, f0_ref, f1_ref, f2_ref, f3_ref, g2_ref, lng_ref, lnb_ref, o_ref):
    f = _group_sum([f0_ref, f1_ref, f2_ref, f3_ref])
    o_ref[...] = _layer_norm(DEEPNORM_ALPHA * x_ref[...] + g2_ref[0, 0] * f, lng_ref[...], lnb_ref[...])


def _final_call(x1, f, mod, ln_g, ln_b, seq_len, tm):
    T = x1.shape[0]
    tok = pl.BlockSpec((tm, D_MODEL), lambda i: (i, 0))
    return pl.pallas_call(
        _final_kernel,
        grid=(T // tm,),
        in_specs=[tok] + _group_specs(tm) + [_mod_spec(mod, 5, tm, max(seq_len // tm, 1)),
                                             _const_spec((1, D_MODEL)), _const_spec((1, D_MODEL))],
        out_specs=tok,
        out_shape=jax.ShapeDtypeStruct((T, D_MODEL), F32),
        compiler_params=_cparams(("arbitrary",)),
        name="final_norm",
    )(x1, f, f, f, f, mod, ln_g.reshape(1, D_MODEL), ln_b.reshape(1, D_MODEL))


def _layer_weights(prm, l):
    c0 = 4 * HG_WIDTH
    c1 = c0 + FOX_WIDTH
    c2 = c1 + FOX_WIDTH
    c3 = c2 + FOX_WIDTH
    c4 = c3 + FOX_HEADS
    w = prm["w_in"][l]
    grouped = lambda a, shp: a.astype(BF16).reshape((N_GROUPS, EXPERTS_PER_GROUP) + shp)
    return {
        "wh": w[:, :c0].astype(BF16),
        "wq": w[:, c0:c1].astype(BF16),
        "wk": w[:, c1:c2].astype(BF16),
        "wv": w[:, c2:c3].astype(BF16),
        "wff": w[:, c3:c4].astype(BF16).astype(F32),
        "wkvf": jnp.concatenate([w[:, c1:c4].astype(BF16),
                                 jnp.zeros((D_MODEL, LANES - FOX_HEADS), BF16)], axis=1),
        "wg": w[:, c4:].astype(BF16),
        "bff": prm["b_fox_f"][l].reshape(1, FOX_HEADS),
        "bfft": prm["b_fox_f"][l].reshape(FOX_HEADS, 1),
        "wa": prm["w_branch_a"][l].astype(BF16),
        "wb": prm["w_branch_b"][l].astype(BF16),
        "wo": prm["w_out"][l].astype(BF16),
        "eg": grouped(prm["w_exp_gate"][l], (D_MODEL, D_EXPERT)),
        "eu": grouped(prm["w_exp_up"][l], (D_MODEL, D_EXPERT)),
        "ed": grouped(prm["w_exp_down"][l], (D_EXPERT, D_MODEL)),
    }


def _trunk(x, mods, hg_state, paged, prm, weights):
    n_seq, seq_len, _ = x.shape
    T = n_seq * seq_len
    prompt = paged is None
    tm = min(256, T)
    win = min(1024, T)
    wr = jnp.pad(prm["w_router"], ((0, 0), (0, LANES - N_EXPERTS)))
    wr1 = wr.astype(BF16)
    wr2 = (wr - wr1.astype(F32)).astype(BF16)
    wr3 = (wr - wr1.astype(F32) - wr2.astype(F32)).astype(BF16)
    w_router_t = jnp.stack([wr1, wr2, wr3])
    if prompt:
        s0_all = jnp.zeros((DEPTH, n_seq, HG_HEADS, HG_DK, HG_DK), F32)
    else:
        s0_all = hg_state.astype(F32)
        cache_k, cache_v, cache_lf, page_table = paged
        n_phys = cache_k.shape[0]
        cache_kt = jnp.transpose(cache_k, (0, 2, 3, 4, 1)).reshape(n_phys, DEPTH, FOX_WIDTH, PAGE_SIZE)
        cache_vt = jnp.transpose(cache_v, (0, 2, 3, 4, 1)).reshape(n_phys, DEPTH, FOX_WIDTH, PAGE_SIZE)
        cache_lft = jnp.transpose(cache_lf, (0, 2, 3, 1))

    ks, vs, lfs, states = [], [], [], []
    kv_bufs = None
    x_in, f_in = x.reshape(T, D_MODEL), None
    for l in range(DEPTH):
        w = weights[l]
        if l == 0:
            ln_g, ln_b, prev_mod = prm["ln_in_g"], prm["ln_in_b"], None
        else:
            ln_g, ln_b, prev_mod = prm["ln2_g"][l - 1], prm["ln2_b"][l - 1], mods[l - 1]
        p = _mixin_call(l, x_in, f_in, prev_mod, ln_g, ln_b, mods[l], w, n_seq, seq_len, tm, prompt, kv_bufs)
        o_a, s_new = _hgrn_call(l, p, prm["hgrn_lower_bounds"], prm["hgrn_norm_g"][l], s0_all[l],
                                n_seq, seq_len)
        if prompt:
            kv_bufs = (p["k"], p["v"], p["lf"])
            o_b = _fox_call(p, n_seq, seq_len, min(512, seq_len))
        else:
            o_b = _decode_call(l, p, cache_kt, cache_vt, cache_lft, page_table, n_seq, seq_len)
            ks.append(p["k"])
            vs.append(p["v"])
            lfs.append(p["lf"])
        x1, h2, cw, route = _merge_call(o_a, o_b, p, mods[l], w, prm["ln1_g"][l], prm["ln1_b"][l],
                                        w_router_t, prm["b_router"], seq_len, min(512, T), win)
        f = _moe_call(h2, cw, route, w, win)
        x_in, f_in = x1, f
        states.append(s_new)
    y = _final_call(x_in, f_in, mods[DEPTH - 1], prm["ln2_g"][DEPTH - 1], prm["ln2_b"][DEPTH - 1], seq_len, tm)
    y = y.reshape(n_seq, seq_len, D_MODEL)
    if prompt:
        kb, vb, lfb = kv_bufs
        k_out = jnp.transpose(kb.reshape(n_seq, DEPTH, FOX_HEADS, FOX_HEAD_DIM, seq_len), (0, 4, 1, 2, 3))
        v_out = jnp.transpose(vb.reshape(n_seq, DEPTH, FOX_HEADS, FOX_HEAD_DIM, seq_len), (0, 4, 1, 2, 3))
        lf_out = jnp.transpose(lfb, (0, 3, 1, 2))
    else:
        k_out = jnp.stack(ks, axis=1).reshape(n_seq, seq_len, DEPTH, FOX_HEADS, FOX_HEAD_DIM)
        v_out = jnp.stack(vs, axis=1).reshape(n_seq, seq_len, DEPTH, FOX_HEADS, FOX_HEAD_DIM)
        lf_out = jnp.stack(lfs, axis=1).reshape(n_seq, seq_len, DEPTH, FOX_HEADS)
    return y, k_out, v_out, lf_out, jnp.stack(states, axis=0)


def kernel(x_prompt, x_sample, c_prompt, c_sample, cache_k, cache_v, cache_logf, state_hgrn, page_table,
           ln_in_g, ln_in_b, w_ada, b_ada, w_in, b_fox_f, hgrn_lower_bounds, hgrn_norm_g,
           w_branch_a, w_branch_b, w_out, ln1_g, ln1_b, w_router, b_router,
           w_exp_gate, w_exp_up, w_exp_down, ln2_g, ln2_b):
    prm = dict(ln_in_g=ln_in_g, ln_in_b=ln_in_b, w_in=w_in, b_fox_f=b_fox_f,
               hgrn_lower_bounds=hgrn_lower_bounds, hgrn_norm_g=hgrn_norm_g, w_branch_a=w_branch_a,
               w_branch_b=w_branch_b, w_out=w_out, ln1_g=ln1_g, ln1_b=ln1_b, w_router=w_router,
               b_router=b_router, w_exp_gate=w_exp_gate, w_exp_up=w_exp_up, w_exp_down=w_exp_down,
               ln2_g=ln2_g, ln2_b=ln2_b)
    n_p, n_s = x_prompt.shape[0], x_sample.shape[0]
    dec_seq = x_sample.shape[1]
    mod_all = _ada_call(jnp.concatenate([c_prompt, c_sample], axis=0), w_ada, b_ada)
    mods_p, mods_s = [], []
    for l in range(DEPTH):
        mp = mod_all[l, :n_p].reshape(n_p, N_MOD, D_MODEL).transpose(1, 0, 2)
        mods_p.append(mp[:, :, None, :])
        ms = mod_all[l, n_p:].reshape(n_s, N_MOD, D_MODEL).transpose(1, 0, 2)
        mods_s.append(jnp.repeat(ms, dec_seq, axis=1)[:, None, :, :])
    weights = [_layer_weights(prm, l) for l in range(DEPTH)]

    y_p, k_p, v_p, lf_p, hg_p = _trunk(x_prompt, mods_p, None, None, prm, weights)
    y_s, k_s, v_s, lf_s, hg_s = _trunk(x_sample, mods_s, state_hgrn,
                                       (cache_k, cache_v, cache_logf, page_table), prm, weights)
    return (y_p, y_s, k_p, v_p, lf_p, hg_p.astype(x_prompt.dtype),
            k_s, v_s, lf_s, hg_s.astype(state_hgrn.dtype))
```

```python
import functools

import jax
import jax.numpy as jnp
import numpy as np
from jax import lax
from jax.experimental import pallas as pl
from jax.experimental.pallas import tpu as pltpu

F32 = jnp.float32
BF16 = jnp.bfloat16

D_MODEL = 1024
DEPTH = 4
PAGE_SIZE = 128
HG_WIDTH = 512
HG_HEADS = 4
HG_DK = 128
HG_CHUNK = 64
HG_SUB = 16
LB_FLOOR = 1e-30
EXP_CLAMP = 80.0
FOX_HEADS = 8
FOX_HEAD_DIM = 64
FOX_WIDTH = 512
MASK_VALUE = -1e30
N_EXPERTS = 16
N_GROUPS = 4
EXPERTS_PER_GROUP = 4
D_EXPERT = 512
N_MOD = 6
DEEPNORM_ALPHA = (2 * DEPTH) ** 0.25
LN_EPS = 1e-5
RMS_EPS = 1e-6
LANES = 128
VMEM_LIMIT = 56 * 1024 * 1024

NN_DIMS = (((1,), (0,)), ((), ()))
NT_DIMS = (((1,), (1,)), ((), ()))
TN_DIMS = (((0,), (0,)), ((), ()))


def _cparams(sem):
    return pltpu.CompilerParams(dimension_semantics=sem, vmem_limit_bytes=VMEM_LIMIT)


def _dot(a, b, dims=NN_DIMS):
    return lax.dot_general(a, b, dims, preferred_element_type=F32)


def _split3(x):
    x1 = x.astype(BF16).astype(F32)
    r1 = x - x1
    x2 = r1.astype(BF16).astype(F32)
    x3 = (r1 - x2).astype(BF16).astype(F32)
    return (x1, x2, x3)


def _dot_sel_lhs(sel, b, dims=NN_DIMS):
    b1, b2, b3 = _split3(b)
    return _dot(sel, b3, dims) + _dot(sel, b2, dims) + _dot(sel, b1, dims)


def _dot_sel_rhs(a, sel, dims=NN_DIMS):
    a1, a2, a3 = _split3(a)
    return _dot(a3, sel, dims) + _dot(a2, sel, dims) + _dot(a1, sel, dims)


def _dot_f32(a, b, dims=NN_DIMS):
    a1, a2, a3 = _split3(a)
    b1, b2, b3 = _split3(b)
    small = _dot(a2, b2, dims) + _dot(a1, b3, dims) + _dot(a3, b1, dims)
    mid = _dot(a1, b2, dims) + _dot(a2, b1, dims)
    return small + mid + _dot(a1, b1, dims)


def _layer_norm(x, g, b):
    xc = x - jnp.mean(x, axis=-1, keepdims=True)
    var = jnp.mean(xc * xc, axis=-1, keepdims=True)
    return xc * lax.rsqrt(var + LN_EPS) * g + b


def _silu(x):
    return x * jax.nn.sigmoid(x)


def _log_sigmoid(x):
    return jnp.minimum(x, 0.0) - jnp.log1p(jnp.exp(-jnp.abs(x)))


def _const_spec(shape):
    nd = len(shape)
    return pl.BlockSpec(shape, lambda *_: (0,) * nd)


def _tri(n, lower):
    r = lax.broadcasted_iota(jnp.int32, (n, n), 0)
    c = lax.broadcasted_iota(jnp.int32, (n, n), 1)
    return jnp.where((r >= c) if lower else (r <= c), 1.0, 0.0).astype(F32)


def _group_specs(tm):
    return [pl.BlockSpec((None, tm, D_MODEL), lambda i, g=g: (g, i, 0)) for g in range(N_GROUPS)]


def _group_sum(refs):
    return (refs[0][...] + refs[1][...]) + (refs[2][...] + refs[3][...])


def _mod_spec(mod, k, tm, tiles_per_seq):
    if mod.shape[2] == 1:
        return pl.BlockSpec((1, 1, 1, D_MODEL), lambda i: (k, i // tiles_per_seq, 0, 0))
    return pl.BlockSpec((1, 1, tm, D_MODEL), lambda i: (k, i, 0, 0))


def _ada_kernel(c_ref, w_ref, b_ref, o_ref):
    s = _silu(c_ref[...]).astype(BF16)
    o_ref[0] = _dot(s, w_ref[0].astype(BF16)) + b_ref[0]


def _ada_call(c_all, w_ada, b_ada):
    n = c_all.shape[0]
    width = N_MOD * D_MODEL
    tn = 1536
    return pl.pallas_call(
        _ada_kernel,
        grid=(DEPTH, width // tn),
        in_specs=[
            pl.BlockSpec((n, D_MODEL), lambda l, j: (0, 0)),
            pl.BlockSpec((1, D_MODEL, tn), lambda l, j: (l, 0, j)),
            pl.BlockSpec((1, 1, tn), lambda l, j: (l, 0, j)),
        ],
        out_specs=pl.BlockSpec((1, n, tn), lambda l, j: (l, 0, j)),
        out_shape=jax.ShapeDtypeStruct((DEPTH, n, width), F32),
        compiler_params=_cparams(("arbitrary", "arbitrary")),
        name="ada_mod",
    )(c_all, w_ada, b_ada.reshape(DEPTH, 1, width))


def _mixin_kernel(has_prev, prompt, n_alias, tiles_per_seq, *refs):
    it = iter(refs)
    xin_ref = next(it)
    if has_prev:
        fin_refs = [next(it) for _ in range(N_GROUPS)]
        g2_ref = next(it)
    lng_ref, lnb_ref, sh_ref, sc_ref = next(it), next(it), next(it), next(it)
    wh_ref, wq_ref, wk_ref, wv_ref, wff_ref, wg_ref = (next(it), next(it), next(it), next(it),
                                                       next(it), next(it))
    bff_ref, bfft_ref = next(it), next(it)
    if prompt:
        tril_ref, triu_ref, plq_ref, plk_ref = next(it), next(it), next(it), next(it)
    for _ in range(n_alias):
        next(it)
    x_ref, hq_ref, hf_ref, hi_ref, hg_ref = next(it), next(it), next(it), next(it), next(it)
    fq_ref, ga_ref, gb_ref = next(it), next(it), next(it)
    k_ref, v_ref, lf_ref = next(it), next(it), next(it)
    if prompt:
        ka_ref, va_ref, carry_c, carry_r = next(it), next(it), next(it), next(it)

    x = xin_ref[...]
    if has_prev:
        x = DEEPNORM_ALPHA * x + g2_ref[0, 0] * _group_sum(fin_refs)
    x = _layer_norm(x, lng_ref[...], lnb_ref[...])
    x_ref[...] = x
    h = (x * (1.0 + sc_ref[0, 0]) + sh_ref[0, 0]).astype(BF16)
    h32 = h.astype(F32)

    ph = _dot(h, wh_ref[...])
    hq_ref[...] = ph[:, 0 * HG_WIDTH:1 * HG_WIDTH]
    hf_ref[...] = ph[:, 1 * HG_WIDTH:2 * HG_WIDTH]
    hi_ref[...] = ph[:, 2 * HG_WIDTH:3 * HG_WIDTH]
    hg_ref[...] = ph[:, 3 * HG_WIDTH:4 * HG_WIDTH]
    q = _dot(h, wq_ref[...]) * (FOX_HEAD_DIM ** -0.5)
    pg = _dot(h, wg_ref[...])
    ga_ref[...] = pg[:, 0:D_MODEL]
    gb_ref[...] = pg[:, D_MODEL:2 * D_MODEL]

    if not prompt:
        fq_ref[...] = q
        k_ref[...] = _dot(h, wk_ref[...])
        v_ref[...] = _dot(h, wv_ref[...])
        lf_ref[...] = _log_sigmoid(_dot(h32, wff_ref[...]) + bff_ref[...])
    else:
        kvf = _dot(h, wk_ref[...])
        kt = kvf[:, 0:FOX_WIDTH].T
        vt = kvf[:, FOX_WIDTH:2 * FOX_WIDTH].T
        k_ref[0, 0] = kt
        v_ref[0, 0] = vt
        ff = kvf[:, 2 * FOX_WIDTH:2 * FOX_WIDTH + LANES]
        lf = _log_sigmoid(ff[:, 0:FOX_HEADS] + bff_ref[...])
        lft = _log_sigmoid(ff.T[0:FOX_HEADS, :] + bfft_ref[...])
        lf_ref[0, 0] = lft

        @pl.when(pl.program_id(0) % tiles_per_seq == 0)
        def _():
            carry_c[...] = jnp.zeros_like(carry_c)
            carry_r[...] = jnp.zeros_like(carry_r)

        fcol = _dot_sel_lhs(tril_ref[...], lf) + carry_c[...]
        frow = _dot_sel_rhs(lft, triu_ref[...]) + carry_r[...]
        tm = lf.shape[0]
        carry_c[...] = fcol[tm - 1:tm, :]
        carry_r[...] = frow[:, tm - 1:tm]

        low = lax.broadcasted_iota(jnp.int32, (1, LANES), 1) < FOX_HEAD_DIM
        q_aug = _dot(jnp.concatenate(list(_split3(fcol)) + [jnp.ones((tm, FOX_HEADS), F32)], axis=1),
                     plq_ref[...])
        k_aug = _dot(plk_ref[...],
                     jnp.concatenate([-r for r in _split3(frow)] + [jnp.ones((FOX_HEADS, tm), F32)], axis=0))
        ones = jnp.ones((FOX_HEAD_DIM, tm), F32)
        for hd in range(FOX_HEADS):
            feat = slice(hd * FOX_HEAD_DIM, (hd + 1) * FOX_HEAD_DIM)
            blk = slice(hd * LANES, (hd + 1) * LANES)
            src = q[:, (hd // 2) * LANES:(hd // 2 + 1) * LANES]
            if hd % 2 == 0:
                fq_ref[:, blk] = jnp.where(low, src, q_aug[:, blk]).astype(BF16)
                ka_ref[0, blk, :] = jnp.concatenate([kt[feat, :], k_aug[feat, :]], axis=0).astype(BF16)
                va_ref[0, blk, :] = jnp.concatenate([vt[feat, :], ones], axis=0).astype(BF16)
            else:
                fq_ref[:, blk] = jnp.where(low, q_aug[:, blk], src).astype(BF16)
                ka_ref[0, blk, :] = jnp.concatenate([k_aug[feat, :], kt[feat, :]], axis=0).astype(BF16)
                va_ref[0, blk, :] = jnp.concatenate([ones, vt[feat, :]], axis=0).astype(BF16)


def _bias_placements():
    plq = np.zeros((4 * FOX_HEADS, FOX_HEADS * LANES), np.float32)
    plk = np.zeros((FOX_HEADS * FOX_HEAD_DIM, 4 * FOX_HEADS), np.float32)
    for hd in range(FOX_HEADS):
        spare = hd * LANES + (FOX_HEAD_DIM if hd % 2 == 0 else 0)
        for j in range(3):
            plq[j * FOX_HEADS + hd, spare + j] = 1.0
            plq[3 * FOX_HEADS + hd, spare + 3 + j] = 1.0
            plk[hd * FOX_HEAD_DIM + j, 3 * FOX_HEADS + hd] = 1.0
            plk[hd * FOX_HEAD_DIM + 3 + j, j * FOX_HEADS + hd] = 1.0
    return jnp.asarray(plq), jnp.asarray(plk)


def _mixin_call(layer, x_in, f_in, prev_mod, ln_g, ln_b, mod, w, n_seq, seq_len, tm, prompt, kv_bufs):
    T = x_in.shape[0]
    tiles_per_seq = max(seq_len // tm, 1)
    tok = lambda width: pl.BlockSpec((tm, width), lambda i: (i, 0))
    has_prev = f_in is not None

    args, specs = [x_in], [tok(D_MODEL)]
    if has_prev:
        args += [f_in] * N_GROUPS + [prev_mod]
        specs += _group_specs(tm) + [_mod_spec(prev_mod, 5, tm, tiles_per_seq)]
    args += [ln_g.reshape(1, D_MODEL), ln_b.reshape(1, D_MODEL), mod, mod]
    specs += [_const_spec((1, D_MODEL)), _const_spec((1, D_MODEL)),
              _mod_spec(mod, 0, tm, tiles_per_seq), _mod_spec(mod, 1, tm, tiles_per_seq)]
    if prompt:
        wnames = ("wh", "wq", "wkvf", "wff", "wff", "wg", "bff", "bfft")
    else:
        wnames = ("wh", "wq", "wk", "wv", "wff", "wg", "bff", "bfft")
    args += [w[n] for n in wnames]
    specs += [_const_spec(w[n].shape) for n in wnames]
    if prompt:
        plq, plk = _bias_placements()
        args += [_tri(tm, True), _tri(tm, False), plq, plk]
        specs += [_const_spec((tm, tm)), _const_spec((tm, tm)), _const_spec(plq.shape), _const_spec(plk.shape)]

    names = ["x", "hq", "hf", "hi", "hg", "fq", "ga", "gb", "k", "v", "lf"]
    widths = [D_MODEL, HG_WIDTH, HG_WIDTH, HG_WIDTH, HG_WIDTH,
              FOX_HEADS * LANES if prompt else FOX_WIDTH, D_MODEL, D_MODEL]
    dtypes = [F32] * 5 + [BF16 if prompt else F32] + [F32] * 2
    out_shape = [jax.ShapeDtypeStruct((T, wd), dt) for wd, dt in zip(widths, dtypes)]
    out_specs = [tok(wd) for wd in widths]
    scratch = []
    aliases = {}
    if prompt:
        fm = lambda rows: pl.BlockSpec((1, 1, rows, tm),
                                       lambda i: (i // tiles_per_seq, layer, 0, i % tiles_per_seq))
        out_shape += [jax.ShapeDtypeStruct((n_seq, DEPTH, FOX_WIDTH, seq_len), F32),
                      jax.ShapeDtypeStruct((n_seq, DEPTH, FOX_WIDTH, seq_len), F32),
                      jax.ShapeDtypeStruct((n_seq, DEPTH, FOX_HEADS, seq_len), F32)]
        out_specs += [fm(FOX_WIDTH), fm(FOX_WIDTH), fm(FOX_HEADS)]
        if kv_bufs is not None:
            for j, buf in enumerate(kv_bufs):
                aliases[len(args)] = 8 + j
                args.append(buf)
                specs.append(pl.BlockSpec(memory_space=pl.ANY))
        names += ["ka", "va"]
        aug = jax.ShapeDtypeStruct((n_seq, FOX_HEADS * LANES, seq_len), BF16)
        aug_spec = pl.BlockSpec((1, FOX_HEADS * LANES, tm),
                                lambda i: (i // tiles_per_seq, 0, i % tiles_per_seq))
        out_shape += [aug, aug]
        out_specs += [aug_spec, aug_spec]
        scratch = [pltpu.VMEM((1, FOX_HEADS), F32), pltpu.VMEM((FOX_HEADS, 1), F32)]
    else:
        out_shape += [jax.ShapeDtypeStruct((T, FOX_WIDTH), F32), jax.ShapeDtypeStruct((T, FOX_WIDTH), F32),
                      jax.ShapeDtypeStruct((T, FOX_HEADS), F32)]
        out_specs += [tok(FOX_WIDTH), tok(FOX_WIDTH), tok(FOX_HEADS)]

    outs = pl.pallas_call(
        functools.partial(_mixin_kernel, has_prev, prompt, len(aliases), tiles_per_seq),
        grid=(T // tm,),
        in_specs=specs,
        out_specs=out_specs,
        out_shape=out_shape,
        scratch_shapes=scratch,
        input_output_aliases=aliases,
        compiler_params=_cparams(("arbitrary",)),
        name="mixer_in",
    )(*args)
    return dict(zip(names, outs))


def _cumsum_rows(x):
    n = x.shape[0]
    row = lax.broadcasted_iota(jnp.int32, x.shape, 0)
    shift = 1
    while shift < n:
        x = x + jnp.where(row >= shift, pltpu.roll(x, shift, 0), 0.0)
        shift *= 2
    return x


def _hgrn_kernel(layer, chunk, sub, n_chunks, hq_ref, hf_ref, hi_ref, hg_ref, lb_ref, ng_ref, s0_ref,
                 o_ref, sout_ref, st_ref):
    t = pl.program_id(1)
    mm = BF16 if chunk >= 16 else F32

    @pl.when(t == 0)
    def _():
        for hd in range(HG_HEADS):
            st_ref[hd] = s0_ref[0, hd].T

    lb_all = lb_ref[...]
    e = jnp.exp(lb_all - jnp.max(lb_all, axis=0, keepdims=True))
    p = e / jnp.sum(e, axis=0, keepdims=True)
    lb = jnp.zeros((1, HG_WIDTH), F32)
    for j in range(1, layer + 1):
        lb = lb + p[j:j + 1, :]
    log_lb = jnp.log(jnp.maximum(lb, LB_FLOOR))
    log1m_lb = jnp.log1p(-lb)
    n_sub = chunk // sub
    causal = (lax.broadcasted_iota(jnp.int32, (chunk, chunk), 1)
              <= lax.broadcasted_iota(jnp.int32, (chunk, chunk), 0))

    def chunk_body(ci, carry):
        r0 = pl.multiple_of(ci * chunk, chunk)
        for hd in range(HG_HEADS):
            ls = slice(hd * HG_DK, (hd + 1) * HG_DK)
            z = hf_ref[pl.ds(r0, chunk), ls]
            q = _silu(hq_ref[pl.ds(r0, chunk), ls])
            v = hi_ref[pl.ds(r0, chunk), ls].astype(mm)
            a = log_lb[:, ls]
            b = log1m_lb[:, ls] + _log_sigmoid(z)
            log_f = jnp.maximum(a, b) + jnp.log1p(jnp.exp(-jnp.abs(a - b)))
            k = (1.0 - lb[:, ls]) * jax.nn.sigmoid(-z)
            cum = _cumsum_rows(log_f)
            blocks = []
            for bi in range(n_sub):
                rows = slice(bi * sub, (bi + 1) * sub)
                base = jnp.zeros((1, HG_DK), F32) if bi == 0 else cum[bi * sub - 1:bi * sub, :]
                k_i = (k * jnp.exp(jnp.minimum(base - cum, EXP_CLAMP))).astype(mm)
                q_i = (q[rows, :] * jnp.exp(cum[rows, :] - base)).astype(mm)
                blocks.append(_dot(q_i, k_i, NT_DIMS))
            attn = blocks[0] if n_sub == 1 else jnp.concatenate(blocks, axis=0)
            attn = jnp.where(causal, attn, 0.0).astype(mm)
            s_t = st_ref[hd]
            o = _dot(attn, v) + _dot((q * jnp.exp(cum)).astype(mm), s_t.astype(mm), NT_DIMS)
            cum_end = cum[chunk - 1:chunk, :]
            k_end = (k * jnp.exp(cum_end - cum)).astype(mm)
            st_ref[hd] = jnp.exp(cum_end) * s_t + _dot(v, k_end, TN_DIMS)
            o = o * lax.rsqrt(jnp.mean(o * o, axis=-1, keepdims=True) + RMS_EPS)
            o_ref[pl.ds(r0, chunk), ls] = o * ng_ref[:, ls] * _silu(hg_ref[pl.ds(r0, chunk), ls])
        return carry

    lax.fori_loop(0, n_chunks, chunk_body, 0, unroll=min(8, n_chunks))

    @pl.when(t == pl.num_programs(1) - 1)
    def _():
        for hd in range(HG_HEADS):
            sout_ref[0, hd] = st_ref[hd].T


def _hgrn_call(layer, p, lower_bounds, norm_g, s0, n_seq, seq_len):
    chunk = HG_CHUNK if seq_len % HG_CHUNK == 0 else seq_len
    sub = min(HG_SUB, chunk)
    tb = min(seq_len, 512)
    nt = seq_len // tb
    tok = pl.BlockSpec((tb, HG_WIDTH), lambda b, t: (b * nt + t, 0))
    st_spec = pl.BlockSpec((1, HG_HEADS, HG_DK, HG_DK), lambda b, t: (b, 0, 0, 0))
    return pl.pallas_call(
        functools.partial(_hgrn_kernel, layer, chunk, sub, tb // chunk),
        grid=(n_seq, nt),
        in_specs=[tok, tok, tok, tok,
                  pl.BlockSpec((DEPTH, HG_WIDTH), lambda b, t: (0, 0)),
                  pl.BlockSpec((1, HG_WIDTH), lambda b, t: (0, 0)),
                  st_spec],
        out_specs=[tok, st_spec],
        out_shape=[jax.ShapeDtypeStruct((n_seq * seq_len, HG_WIDTH), F32),
                   jax.ShapeDtypeStruct((n_seq, HG_HEADS, HG_DK, HG_DK), F32)],
        scratch_shapes=[pltpu.VMEM((HG_HEADS, HG_DK, HG_DK), F32)],
        compiler_params=_cparams(("arbitrary", "arbitrary")),
        name="hgrn",
    )(p["hq"], p["hf"], p["hi"], p["hg"], lower_bounds, norm_g.reshape(1, HG_WIDTH), s0)


def _fox_kernel(tq, tk, qi_ref, ki_ref, q_ref, k_ref, v_ref, o_ref, m_ref, acc_ref):
    qi = qi_ref[pl.program_id(1)]
    ki = ki_ref[pl.program_id(1)]

    @pl.when(ki == 0)
    def _():
        m_ref[...] = jnp.full_like(m_ref, MASK_VALUE)
        acc_ref[...] = jnp.zeros_like(acc_ref)

    def absorb(diagonal):
        if diagonal:
            visible = (lax.broadcasted_iota(jnp.int32, (tq, tk), 1)
                       <= lax.broadcasted_iota(jnp.int32, (tq, tk), 0))
        for hd in range(FOX_HEADS):
            blk = slice(hd * LANES, (hd + 1) * LANES)
            s = _dot(q_ref[:, blk], k_ref[0, blk, :])
            if diagonal:
                s = jnp.where(visible, s, MASK_VALUE)
            cols = [s[:, c * LANES:(c + 1) * LANES] for c in range(tk // LANES)]
            cmax = cols[0]
            for c in cols[1:]:
                cmax = jnp.maximum(cmax, c)
            m_old = m_ref[hd]
            m_new = jnp.maximum(m_old, jnp.broadcast_to(jnp.max(cmax, axis=-1, keepdims=True), (tq, LANES)))
            pe = jnp.concatenate([jnp.exp(c - m_new) for c in cols], axis=1).astype(BF16)
            m_ref[hd] = m_new
            acc_ref[:, blk] = jnp.exp(m_old - m_new) * acc_ref[:, blk] + _dot(pe, v_ref[0, blk, :], NT_DIMS)

    @pl.when(ki < qi)
    def _():
        absorb(False)

    @pl.when(ki == qi)
    def _():
        absorb(True)
        low = lax.broadcasted_iota(jnp.int32, (1, LANES), 1) < FOX_HEAD_DIM
        for pr in range(FOX_HEADS // 2):
            even = acc_ref[:, 2 * pr * LANES:(2 * pr + 1) * LANES]
            odd = acc_ref[:, (2 * pr + 1) * LANES:(2 * pr + 2) * LANES]
            o_ref[:, pr * LANES:(pr + 1) * LANES] = jnp.where(
                low, even / pltpu.roll(even, FOX_HEAD_DIM, 1), odd / pltpu.roll(odd, FOX_HEAD_DIM, 1))


def _fox_call(p, n_seq, seq_len, tq):
    tk = tq
    nq = seq_len // tq
    T = n_seq * seq_len
    width = FOX_HEADS * LANES
    pairs = [(qi, ki) for qi in range(nq) for ki in range(qi + 1)]
    qi_tab = jnp.asarray([pr[0] for pr in pairs], jnp.int32)
    ki_tab = jnp.asarray([pr[1] for pr in pairs], jnp.int32)
    kspec = pl.BlockSpec((1, width, tk), lambda b, s, qt, kt: (b, 0, kt[s]))
    grid_spec = pltpu.PrefetchScalarGridSpec(
        num_scalar_prefetch=2,
        grid=(n_seq, len(pairs)),
        in_specs=[pl.BlockSpec((tq, width), lambda b, s, qt, kt: (b * nq + qt[s], 0)), kspec, kspec],
        out_specs=pl.BlockSpec((tq, FOX_WIDTH), lambda b, s, qt, kt: (b * nq + qt[s], 0)),
        scratch_shapes=[pltpu.VMEM((FOX_HEADS, tq, LANES), F32), pltpu.VMEM((tq, width), F32)],
    )
    return pl.pallas_call(
        functools.partial(_fox_kernel, tq, tk),
        grid_spec=grid_spec,
        out_shape=jax.ShapeDtypeStruct((T, FOX_WIDTH), F32),
        compiler_params=_cparams(("arbitrary", "arbitrary")),
        name="fox_prompt",
    )(qi_tab, ki_tab, p["fq"], p["ka"], p["va"])


PAGES_PER_STEP = 16


def _decode_kernel(n_new, pt_ref, q_ref, kn_ref, vn_ref, lfn_ref, *refs):
    npg = PAGES_PER_STEP
    k_refs = refs[0:npg]
    v_refs = refs[npg:2 * npg]
    lf_refs = refs[2 * npg:3 * npg]
    o_ref, m_ref, l_ref, acc_ref, carry_ref = refs[3 * npg:]
    j = pl.program_id(1)
    rows = FOX_HEADS * n_new

    @pl.when(j == 0)
    def _():
        m_ref[...] = jnp.full_like(m_ref, MASK_VALUE)
        l_ref[...] = jnp.zeros_like(l_ref)
        acc_ref[...] = jnp.zeros_like(acc_ref)
        carry_ref[...] = jnp.zeros_like(carry_ref)

    q = q_ref[...]
    rr = lax.broadcasted_iota(jnp.int32, (rows, FOX_WIDTH), 0)
    cc = lax.broadcasted_iota(jnp.int32, (rows, FOX_WIDTH), 1)
    q_rep = jnp.concatenate([q] * FOX_HEADS, axis=0)
    qbd = jnp.where((rr // n_new) == (cc // FOX_HEAD_DIM), q_rep, 0.0)
    qbd_bf = qbd.astype(BF16)
    ehe = jnp.where(lax.broadcasted_iota(jnp.int32, (rows, FOX_HEADS), 0) // n_new
                    == lax.broadcasted_iota(jnp.int32, (rows, FOX_HEADS), 1), 1.0, 0.0)
    triu = _tri(PAGE_SIZE, False)

    def absorb(s, pv_fn):
        m_old = m_ref[...]
        m_new = jnp.maximum(m_old, jnp.max(s, axis=-1, keepdims=True))
        alpha = jnp.exp(m_old - m_new)
        pe = jnp.exp(s - m_new)
        l_ref[...] = alpha * l_ref[...] + jnp.sum(pe, axis=-1, keepdims=True)
        m_ref[...] = m_new
        acc_ref[...] = alpha * acc_ref[...] + pv_fn(pe)

    lf_all = jnp.concatenate([lf_refs[pg][...] for pg in range(npg)], axis=0)
    within = _dot_sel_rhs(lf_all, triu)
    nr = npg * FOX_HEADS
    ri = lax.broadcasted_iota(jnp.int32, (nr, nr), 0)
    ci = lax.broadcasted_iota(jnp.int32, (nr, nr), 1)
    earlier = jnp.where(jnp.logical_and(ri % FOX_HEADS == ci % FOX_HEADS, ci < ri), 1.0, 0.0)
    totals = jnp.broadcast_to(within[:, PAGE_SIZE - 1:PAGE_SIZE], (nr, PAGE_SIZE))
    carry_rep = jnp.concatenate([jnp.broadcast_to(carry_ref[...], (FOX_HEADS, PAGE_SIZE))] * npg, axis=0)
    f_all = within + _dot_sel_lhs(earlier, totals) + carry_rep
    carry_ref[...] = f_all[nr - FOX_HEADS:nr, PAGE_SIZE - 1:PAGE_SIZE]
    bias = jnp.concatenate(
        [jnp.concatenate([jnp.broadcast_to(f_all[pg * FOX_HEADS + hd:pg * FOX_HEADS + hd + 1, :],
                                           (n_new, PAGE_SIZE)) for hd in range(FOX_HEADS)], axis=0)
         for pg in range(npg)], axis=1)
    kt = jnp.concatenate([k_refs[pg][...].astype(BF16) for pg in range(npg)], axis=1)
    vt = jnp.concatenate([v_refs[pg][...].astype(BF16) for pg in range(npg)], axis=1)
    absorb(_dot(qbd_bf, kt) - bias, lambda pe: _dot(pe.astype(BF16), vt, NT_DIMS))

    @pl.when(j == pl.num_programs(1) - 1)
    def _():
        kn = kn_ref[...]
        vn = vn_ref[...]
        eye = (lax.broadcasted_iota(jnp.int32, (FOX_HEADS, FOX_HEADS), 0)
               == lax.broadcasted_iota(jnp.int32, (FOX_HEADS, FOX_HEADS), 1))
        carry_row = jnp.sum(jnp.where(eye, carry_ref[...], 0.0), axis=0, keepdims=True)
        f_new = _dot_sel_lhs(_tri(n_new, True), lfn_ref[...]) + carry_row
        s = _dot(qbd, kn, NT_DIMS) - _dot_sel_lhs(ehe, f_new, NT_DIMS)
        key_i = lax.broadcasted_iota(jnp.int32, (rows, n_new), 1)
        qry_i = lax.broadcasted_iota(jnp.int32, (rows, n_new), 0) % n_new
        s = jnp.where(key_i <= qry_i, s, MASK_VALUE)
        absorb(s, lambda pe: _dot(pe, vn))
        out = acc_ref[...] / l_ref[...]
        lane_head = lax.broadcasted_iota(jnp.int32, (n_new, FOX_WIDTH), 1) // FOX_HEAD_DIM
        res = jnp.zeros((n_new, FOX_WIDTH), F32)
        for hd in range(FOX_HEADS):
            res = res + jnp.where(lane_head == hd, out[hd * n_new:(hd + 1) * n_new, :], 0.0)
        o_ref[...] = res


def _decode_call(layer, p, cache_kt, cache_vt, cache_lft, page_table, n_seq, n_new):
    n_pages = page_table.shape[1]
    npg = PAGES_PER_STEP
    pt_flat = page_table.reshape(-1).astype(jnp.int32)

    def page_map(pg):
        return lambda n, j, pt: (pt[n * n_pages + j * npg + pg], layer, 0, 0)

    new_w = pl.BlockSpec((n_new, FOX_WIDTH), lambda n, j, pt: (n, 0))
    in_specs = [new_w, new_w, new_w, pl.BlockSpec((n_new, FOX_HEADS), lambda n, j, pt: (n, 0))]
    in_specs += [pl.BlockSpec((None, None, FOX_WIDTH, PAGE_SIZE), page_map(pg)) for pg in range(npg)]
    in_specs += [pl.BlockSpec((None, None, FOX_WIDTH, PAGE_SIZE), page_map(pg)) for pg in range(npg)]
    in_specs += [pl.BlockSpec((None, None, FOX_HEADS, PAGE_SIZE), page_map(pg)) for pg in range(npg)]
    rows = FOX_HEADS * n_new
    grid_spec = pltpu.PrefetchScalarGridSpec(
        num_scalar_prefetch=1,
        grid=(n_seq, n_pages // npg),
        in_specs=in_specs,
        out_specs=new_w,
        scratch_shapes=[pltpu.VMEM((rows, 1), F32), pltpu.VMEM((rows, 1), F32),
                        pltpu.VMEM((rows, FOX_WIDTH), F32), pltpu.VMEM((FOX_HEADS, 1), F32)],
    )
    return pl.pallas_call(
        functools.partial(_decode_kernel, n_new),
        grid_spec=grid_spec,
        out_shape=jax.ShapeDtypeStruct((n_seq * n_new, FOX_WIDTH), F32),
        compiler_params=_cparams(("arbitrary", "arbitrary")),
        name="fox_decode",
    )(pt_flat, p["fq"], p["k"], p["v"], p["lf"],
      *([cache_kt] * npg), *([cache_vt] * npg), *([cache_lft] * npg))


def _merge_kernel(tiles_per_win, part, oa_ref, ob_ref, ga_ref, gb_ref, x_ref, g1_ref, sh_ref, sc_ref,
                  wa_ref, wb_ref, wo_ref, lng_ref, lnb_ref, wrt_ref, br_ref, triu_ref,
                  x1_ref, h2_ref, cw_ref, route_ref, carry_ref):
    @pl.when(pl.program_id(0) % tiles_per_win == 0)
    def _():
        carry_ref[...] = jnp.zeros_like(carry_ref)

    for pi in range(x_ref.shape[0] // part):
        _merge_part(slice(pi * part, (pi + 1) * part), oa_ref, ob_ref, ga_ref, gb_ref, x_ref, g1_ref,
                    sh_ref, sc_ref, wa_ref, wb_ref, wo_ref, lng_ref, lnb_ref, wrt_ref, br_ref, triu_ref,
                    x1_ref, h2_ref, cw_ref, route_ref, carry_ref)


def _merge_part(rs, oa_ref, ob_ref, ga_ref, gb_ref, x_ref, g1_ref, sh_ref, sc_ref,
                wa_ref, wb_ref, wo_ref, lng_ref, lnb_ref, wrt_ref, br_ref, triu_ref,
                x1_ref, h2_ref, cw_ref, route_ref, carry_ref):
    def mod_rows(ref):
        return ref[0, 0] if ref.shape[2] == 1 else ref[0, 0, rs, :]

    ya = _dot(oa_ref[rs, :].astype(BF16), wa_ref[...])
    yb = _dot(ob_ref[rs, :].astype(BF16), wb_ref[...])
    merged = jax.nn.sigmoid(ga_ref[rs, :]) * ya + jax.nn.sigmoid(gb_ref[rs, :]) * yb
    m = _dot(merged.astype(BF16), wo_ref[...])
    x1 = _layer_norm(DEEPNORM_ALPHA * x_ref[rs, :] + mod_rows(g1_ref) * m, lng_ref[...], lnb_ref[...])
    x1_ref[rs, :] = x1
    h2 = x1 * (1.0 + mod_rows(sc_ref)) + mod_rows(sh_ref)
    h2_ref[rs, :] = h2.astype(BF16)

    tm = h2.shape[0]
    hp = [t.astype(BF16) for t in _split3(h2)]
    wr = [wrt_ref[j] for j in range(3)]
    small = _dot(hp[1], wr[1]) + _dot(hp[0], wr[2]) + _dot(hp[2], wr[0])
    mid = _dot(hp[0], wr[1]) + _dot(hp[1], wr[0])
    logits = (small + mid + _dot(hp[0], wr[0])).T[:N_EXPERTS, :] + br_ref[...]
    ex = jnp.exp(logits - jnp.max(logits, axis=0, keepdims=True))
    scores = ex / jnp.sum(ex, axis=0, keepdims=True)
    gs = []
    for g in range(N_GROUPS):
        r = [scores[g * EXPERTS_PER_GROUP + e:g * EXPERTS_PER_GROUP + e + 1, :]
             for e in range(EXPERTS_PER_GROUP)]
        best = r[0] + r[1]
        for a in range(EXPERTS_PER_GROUP):
            for b in range(a + 1, EXPERTS_PER_GROUP):
                best = jnp.maximum(best, r[a] + r[b])
        gs.append(best)
    gmax = jnp.maximum(jnp.maximum(gs[0], gs[1]), jnp.maximum(gs[2], gs[3]))
    gid = jnp.where(gs[0] == gmax, 0, jnp.where(gs[1] == gmax, 1, jnp.where(gs[2] == gmax, 2, 3)))
    erow = lax.broadcasted_iota(jnp.int32, (N_EXPERTS, tm), 0)
    masked = jnp.where(erow // EXPERTS_PER_GROUP == gid, scores, -1.0)
    top1 = jnp.max(masked, axis=0, keepdims=True)
    idx1 = jnp.min(jnp.where(masked == top1, erow, N_EXPERTS), axis=0, keepdims=True)
    masked2 = jnp.where(erow == idx1, -2.0, masked)
    top2 = jnp.max(masked2, axis=0, keepdims=True)
    idx2 = jnp.min(jnp.where(masked2 == top2, erow, N_EXPERTS), axis=0, keepdims=True)
    den = top1 + top2
    cw_ref[:, rs] = jnp.where(erow == idx1, top1 / den, 0.0) + jnp.where(erow == idx2, top2 / den, 0.0)

    grow = lax.broadcasted_iota(jnp.int32, (8, tm), 0)
    member = grow == gid
    incl = _dot(jnp.where(member, 1.0, 0.0), triu_ref[...]) + carry_ref[...]
    carry_ref[...] = incl[:, tm - 1:tm]
    rank = jnp.sum(jnp.where(member, incl, 0.0), axis=0, keepdims=True) - 1.0
    route_ref[:, rs] = jnp.where(grow == 0, gid, jnp.where(grow == 1, rank.astype(jnp.int32), 0))


def _merge_call(o_a, o_b, p, mod, w, ln_g, ln_b, w_router_t, b_router, seq_len, tm, win):
    T = o_a.shape[0]
    part = min(tm, 256)
    tiles_per_seq = max(seq_len // tm, 1)
    tok = lambda width: pl.BlockSpec((tm, width), lambda i: (i, 0))
    return pl.pallas_call(
        functools.partial(_merge_kernel, win // tm, part),
        grid=(T // tm,),
        in_specs=[tok(HG_WIDTH), tok(FOX_WIDTH), tok(D_MODEL), tok(D_MODEL), tok(D_MODEL),
                  _mod_spec(mod, 2, tm, tiles_per_seq), _mod_spec(mod, 3, tm, tiles_per_seq),
                  _mod_spec(mod, 4, tm, tiles_per_seq),
                  _const_spec((HG_WIDTH, D_MODEL)), _const_spec((FOX_WIDTH, D_MODEL)),
                  _const_spec((D_MODEL, D_MODEL)), _const_spec((1, D_MODEL)), _const_spec((1, D_MODEL)),
                  _const_spec((3, D_MODEL, LANES)), _const_spec((N_EXPERTS, 1)), _const_spec((part, part))],
        out_specs=[tok(D_MODEL), tok(D_MODEL),
                   pl.BlockSpec((N_EXPERTS, tm), lambda i: (0, i)),
                   pl.BlockSpec((8, tm), lambda i: (0, i))],
        out_shape=[jax.ShapeDtypeStruct((T, D_MODEL), F32), jax.ShapeDtypeStruct((T, D_MODEL), BF16),
                   jax.ShapeDtypeStruct((N_EXPERTS, T), F32), jax.ShapeDtypeStruct((8, T), jnp.int32)],
        scratch_shapes=[pltpu.VMEM((8, 1), F32)],
        compiler_params=_cparams(("arbitrary",)),
        name="merge",
    )(o_a, o_b, p["ga"], p["gb"], p["x"], mod, mod, mod, w["wa"], w["wb"], w["wo"],
      ln_g.reshape(1, D_MODEL), ln_b.reshape(1, D_MODEL), w_router_t, b_router.reshape(N_EXPERTS, 1),
      _tri(part, False))


MOE_UNIT = 64
MOE_MAX_UNITS = 8


def _moe_kernel(h_ref, cw_ref, route_ref, wg_ref, wu_ref, wd_ref, y_ref):
    g = pl.program_id(0)
    win = h_ref.shape[0]
    y_ref[...] = jnp.zeros_like(y_ref)

    gid = route_ref[0:1, :]
    rank = route_ref[1:2, :]
    member = gid == g
    count = jnp.sum(jnp.where(member, 1, 0))

    def process(base, rows):
        slot = lax.broadcasted_iota(jnp.int32, (rows, win), 0)
        ecol = lax.broadcasted_iota(jnp.int32, (rows, N_EXPERTS), 1)
        perm_f = jnp.where(jnp.logical_and(member, rank - base == slot), 1.0, 0.0)
        perm = perm_f.astype(BF16)
        hs = _dot(perm, h_ref[...]).astype(BF16)
        cwg = _dot_sel_lhs(perm_f, cw_ref[...], NT_DIMS)
        acc = jnp.zeros((rows, D_MODEL), F32)
        for e in range(EXPERTS_PER_GROUP):
            cwe = jnp.sum(jnp.where(ecol == g * EXPERTS_PER_GROUP + e, cwg, 0.0), axis=1, keepdims=True)
            a = _dot(hs, wg_ref[0, e])
            u = _dot(hs, wu_ref[0, e])
            hid = (_silu(a) * u * cwe).astype(BF16)
            acc = acc + _dot(hid, wd_ref[0, e])
        y_ref[...] += _dot(perm, acc.astype(BF16), TN_DIMS)

    big = MOE_UNIT * MOE_MAX_UNITS
    n_big = count // big

    def big_pass(j, carry):
        process(j * big, big)
        return carry

    lax.fori_loop(0, n_big, big_pass, 0)
    rest_units = (count - n_big * big + MOE_UNIT - 1) // MOE_UNIT
    for units in range(1, MOE_MAX_UNITS + 1):
        @pl.when(rest_units == units)
        def _(units=units):
            process(n_big * big, units * MOE_UNIT)


def _moe_call(h2, cw, route, w, win):
    T = h2.shape[0]
    wspec = lambda shp: pl.BlockSpec((1,) + shp, lambda g, i: (g, 0, 0, 0))
    return pl.pallas_call(
        _moe_kernel,
        grid=(N_GROUPS, T // win),
        in_specs=[pl.BlockSpec((win, D_MODEL), lambda g, i: (i, 0)),
                  pl.BlockSpec((N_EXPERTS, win), lambda g, i: (0, i)),
                  pl.BlockSpec((8, win), lambda g, i: (0, i)),
                  wspec((EXPERTS_PER_GROUP, D_MODEL, D_EXPERT)),
                  wspec((EXPERTS_PER_GROUP, D_MODEL, D_EXPERT)),
                  wspec((EXPERTS_PER_GROUP, D_EXPERT, D_MODEL))],
        out_specs=pl.BlockSpec((None, win, D_MODEL), lambda g, i: (g, i, 0)),
        out_shape=jax.ShapeDtypeStruct((N_GROUPS, T, D_MODEL), F32),
        compiler_params=_cparams(("arbitrary", "arbitrary")),
        name="moe",
    )(h2, cw, route, w["eg"], w["eu"], w["ed"])


def _final_kernel(x_ref, f0_ref, f1_ref, f2_ref, f3_ref, g2_ref, lng_ref, lnb_ref, o_ref):
    f = _group_sum([f0_ref, f1_ref, f2_ref, f3_ref])
    o_ref[...] = _layer_norm(DEEPNORM_ALPHA * x_ref[...] + g2_ref[0, 0] * f, lng_ref[...], lnb_ref[...])


def _final_call(x1, f, mod, ln_g, ln_b, seq_len, tm):
    T = x1.shape[0]
    tok = pl.BlockSpec((tm, D_MODEL), lambda i: (i, 0))
    return pl.pallas_call(
        _final_kernel,
        grid=(T // tm,),
        in_specs=[tok] + _group_specs(tm) + [_mod_spec(mod, 5, tm, max(seq_len // tm, 1)),
                                             _const_spec((1, D_MODEL)), _const_spec((1, D_MODEL))],
        out_specs=tok,
        out_shape=jax.ShapeDtypeStruct((T, D_MODEL), F32),
        compiler_params=_cparams(("arbitrary",)),
        name="final_norm",
    )(x1, f, f, f, f, mod, ln_g.reshape(1, D_MODEL), ln_b.reshape(1, D_MODEL))


def _layer_weights(prm, l):
    c0 = 4 * HG_WIDTH
    c1 = c0 + FOX_WIDTH
    c2 = c1 + FOX_WIDTH
    c3 = c2 + FOX_WIDTH
    c4 = c3 + FOX_HEADS
    w = prm["w_in"][l]
    grouped = lambda a, shp: a.astype(BF16).reshape((N_GROUPS, EXPERTS_PER_GROUP) + shp)
    return {
        "wh": w[:, :c0].astype(BF16),
        "wq": w[:, c0:c1].astype(BF16),
        "wk": w[:, c1:c2].astype(BF16),
        "wv": w[:, c2:c3].astype(BF16),
        "wff": w[:, c3:c4].astype(BF16).astype(F32),
        "wkvf": jnp.concatenate([w[:, c1:c4].astype(BF16),
                                 jnp.zeros((D_MODEL, LANES - FOX_HEADS), BF16)], axis=1),
        "wg": w[:, c4:].astype(BF16),
        "bff": prm["b_fox_f"][l].reshape(1, FOX_HEADS),
        "bfft": prm["b_fox_f"][l].reshape(FOX_HEADS, 1),
        "wa": prm["w_branch_a"][l].astype(BF16),
        "wb": prm["w_branch_b"][l].astype(BF16),
        "wo": prm["w_out"][l].astype(BF16),
        "eg": grouped(prm["w_exp_gate"][l], (D_MODEL, D_EXPERT)),
        "eu": grouped(prm["w_exp_up"][l], (D_MODEL, D_EXPERT)),
        "ed": grouped(prm["w_exp_down"][l], (D_EXPERT, D_MODEL)),
    }


def _trunk(x, mods, hg_state, paged, prm, weights):
    n_seq, seq_len, _ = x.shape
    T = n_seq * seq_len
    prompt = paged is None
    tm = min(256, T)
    win = min(1024, T)
    wr = jnp.pad(prm["w_router"], ((0, 0), (0, LANES - N_EXPERTS)))
    wr1 = wr.astype(BF16)
    wr2 = (wr - wr1.astype(F32)).astype(BF16)
    wr3 = (wr - wr1.astype(F32) - wr2.astype(F32)).astype(BF16)
    w_router_t = jnp.stack([wr1, wr2, wr3])
    if prompt:
        s0_all = jnp.zeros((DEPTH, n_seq, HG_HEADS, HG_DK, HG_DK), F32)
    else:
        s0_all = hg_state.astype(F32)
        cache_k, cache_v, cache_lf, page_table = paged
        n_phys = cache_k.shape[0]
        cache_kt = jnp.transpose(cache_k, (0, 2, 3, 4, 1)).reshape(n_phys, DEPTH, FOX_WIDTH, PAGE_SIZE)
        cache_vt = jnp.transpose(cache_v, (0, 2, 3, 4, 1)).reshape(n_phys, DEPTH, FOX_WIDTH, PAGE_SIZE)
        cache_lft = jnp.transpose(cache_lf, (0, 2, 3, 1))

    ks, vs, lfs, states = [], [], [], []
    kv_bufs = None
    x_in, f_in = x.reshape(T, D_MODEL), None
    for l in range(DEPTH):
        w = weights[l]
        if l == 0:
            ln_g, ln_b, prev_mod = prm["ln_in_g"], prm["ln_in_b"], None
        else:
            ln_g, ln_b, prev_mod = prm["ln2_g"][l - 1], prm["ln2_b"][l - 1], mods[l - 1]
        p = _mixin_call(l, x_in, f_in, prev_mod, ln_g, ln_b, mods[l], w, n_seq, seq_len, tm, prompt, kv_bufs)
        o_a, s_new = _hgrn_call(l, p, prm["hgrn_lower_bounds"], prm["hgrn_norm_g"][l], s0_all[l],
                                n_seq, seq_len)
        if prompt:
            kv_bufs = (p["k"], p["v"], p["lf"])
            o_b = _fox_call(p, n_seq, seq_len, min(512, seq_len))
        else:
            o_b = _decode_call(l, p, cache_kt, cache_vt, cache_lft, page_table, n_seq, seq_len)
            ks.append(p["k"])
            vs.append(p["v"])
            lfs.append(p["lf"])
        x1, h2, cw, route = _merge_call(o_a, o_b, p, mods[l], w, prm["ln1_g"][l], prm["ln1_b"][l],
                                        w_router_t, prm["b_router"], seq_len, min(512, T), win)
        f = _moe_call(h2, cw, route, w, win)
        x_in, f_in = x1, f
        states.append(s_new)
    y = _final_call(x_in, f_in, mods[DEPTH - 1], prm["ln2_g"][DEPTH - 1], prm["ln2_b"][DEPTH - 1], seq_len, tm)
    y = y.reshape(n_seq, seq_len, D_MODEL)
    if prompt:
        kb, vb, lfb = kv_bufs
        k_out = jnp.transpose(kb.reshape(n_seq, DEPTH, FOX_HEADS, FOX_HEAD_DIM, seq_len), (0, 4, 1, 2, 3))
        v_out = jnp.transpose(vb.reshape(n_seq, DEPTH, FOX_HEADS, FOX_HEAD_DIM, seq_len), (0, 4, 1, 2, 3))
        lf_out = jnp.transpose(lfb, (0, 3, 1, 2))
    else:
        k_out = jnp.stack(ks, axis=1).reshape(n_seq, seq_len, DEPTH, FOX_HEADS, FOX_HEAD_DIM)
        v_out = jnp.stack(vs, axis=1).reshape(n_seq, seq_len, DEPTH, FOX_HEADS, FOX_HEAD_DIM)
        lf_out = jnp.stack(lfs, axis=1).reshape(n_seq, seq_len, DEPTH, FOX_HEADS)
    return y, k_out, v_out, lf_out, jnp.stack(states, axis=0)


def kernel(x_prompt, x_sample, c_prompt, c_sample, cache_k, cache_v, cache_logf, state_hgrn, page_table,
           ln_in_g, ln_in_b, w_ada, b_ada, w_in, b_fox_f, hgrn_lower_bounds, hgrn_norm_g,
           w_branch_a, w_branch_b, w_out, ln1_g, ln1_b, w_router, b_router,
           w_exp_gate, w_exp_up, w_exp_down, ln2_g, ln2_b):
    prm = dict(ln_in_g=ln_in_g, ln_in_b=ln_in_b, w_in=w_in, b_fox_f=b_fox_f,
               hgrn_lower_bounds=hgrn_lower_bounds, hgrn_norm_g=hgrn_norm_g, w_branch_a=w_branch_a,
               w_branch_b=w_branch_b, w_out=w_out, ln1_g=ln1_g, ln1_b=ln1_b, w_router=w_router,
               b_router=b_router, w_exp_gate=w_exp_gate, w_exp_up=w_exp_up, w_exp_down=w_exp_down,
               ln2_g=ln2_g, ln2_b=ln2_b)
    n_p, n_s = x_prompt.shape[0], x_sample.shape[0]
    dec_seq = x_sample.shape[1]
    mod_all = _ada_call(jnp.concatenate([c_prompt, c_sample], axis=0), w_ada, b_ada)
    mods_p, mods_s = [], []
    for l in range(DEPTH):
        mp = mod_all[l, :n_p].reshape(n_p, N_MOD, D_MODEL).transpose(1, 0, 2)
        mods_p.append(mp[:, :, None, :])
        ms = mod_all[l, n_p:].reshape(n_s, N_MOD, D_MODEL).transpose(1, 0, 2)
        mods_s.append(jnp.repeat(ms, dec_seq, axis=1)[:, None, :, :])
    weights = [_layer_weights(prm, l) for l in range(DEPTH)]

    y_p, k_p, v_p, lf_p, hg_p = _trunk(x_prompt, mods_p, None, None, prm, weights)
    y_s, k_s, v_s, lf_s, hg_s = _trunk(x_sample, mods_s, state_hgrn,
                                       (cache_k, cache_v, cache_logf, page_table), prm, weights)
    return (y_p, y_s, k_p, v_p, lf_p, hg_p.astype(x_prompt.dtype),
            k_s, v_s, lf_s, hg_s.astype(state_hgrn.dtype))
```

```python
import functools

import jax
import jax.numpy as jnp
import numpy as np
from jax import lax
from jax.experimental import pallas as pl
from jax.experimental.pallas import tpu as pltpu

F32 = jnp.float32
BF16 = jnp.bfloat16

D_MODEL = 1024
DEPTH = 4
PAGE_SIZE = 128
HG_WIDTH = 512
HG_HEADS = 4
HG_DK = 128
HG_CHUNK = 64
HG_SUB = 16
LB_FLOOR = 1e-30
EXP_CLAMP = 80.0
FOX_HEADS = 8
FOX_HEAD_DIM = 64
FOX_WIDTH = 512
MASK_VALUE = -1e30
N_EXPERTS = 16
N_GROUPS = 4
EXPERTS_PER_GROUP = 4
D_EXPERT = 512
N_MOD = 6
DEEPNORM_ALPHA = (2 * DEPTH) ** 0.25
LN_EPS = 1e-5
RMS_EPS = 1e-6
LANES = 128
VMEM_LIMIT = 56 * 1024 * 1024

NN_DIMS = (((1,), (0,)), ((), ()))
NT_DIMS = (((1,), (1,)), ((), ()))
TN_DIMS = (((0,), (0,)), ((), ()))


def _cparams(sem):
    return pltpu.CompilerParams(dimension_semantics=sem, vmem_limit_bytes=VMEM_LIMIT)


def _dot(a, b, dims=NN_DIMS):
    return lax.dot_general(a, b, dims, preferred_element_type=F32)


def _split3(x):
    x1 = x.astype(BF16).astype(F32)
    r1 = x - x1
    x2 = r1.astype(BF16).astype(F32)
    x3 = (r1 - x2).astype(BF16).astype(F32)
    return (x1, x2, x3)


def _dot_sel_lhs(sel, b, dims=NN_DIMS):
    b1, b2, b3 = _split3(b)
    return _dot(sel, b3, dims) + _dot(sel, b2, dims) + _dot(sel, b1, dims)


def _dot_sel_rhs(a, sel, dims=NN_DIMS):
    a1, a2, a3 = _split3(a)
    return _dot(a3, sel, dims) + _dot(a2, sel, dims) + _dot(a1, sel, dims)


def _dot_f32(a, b, dims=NN_DIMS):
    a1, a2, a3 = _split3(a)
    b1, b2, b3 = _split3(b)
    small = _dot(a2, b2, dims) + _dot(a1, b3, dims) + _dot(a3, b1, dims)
    mid = _dot(a1, b2, dims) + _dot(a2, b1, dims)
    return small + mid + _dot(a1, b1, dims)


def _layer_norm(x, g, b):
    xc = x - jnp.mean(x, axis=-1, keepdims=True)
    var = jnp.mean(xc * xc, axis=-1, keepdims=True)
    return xc * lax.rsqrt(var + LN_EPS) * g + b


def _silu(x):
    return x * jax.nn.sigmoid(x)


def _log_sigmoid(x):
    return jnp.minimum(x, 0.0) - jnp.log1p(jnp.exp(-jnp.abs(x)))


def _const_spec(shape):
    nd = len(shape)
    return pl.BlockSpec(shape, lambda *_: (0,) * nd)


def _tri(n, lower):
    r = lax.broadcasted_iota(jnp.int32, (n, n), 0)
    c = lax.broadcasted_iota(jnp.int32, (n, n), 1)
    return jnp.where((r >= c) if lower else (r <= c), 1.0, 0.0).astype(F32)


def _mod_spec(mod, k, tm, tiles_per_seq):
    if mod.shape[2] == 1:
        return pl.BlockSpec((1, 1, 1, D_MODEL), lambda i: (k, i // tiles_per_seq, 0, 0))
    return pl.BlockSpec((1, 1, tm, D_MODEL), lambda i: (k, i, 0, 0))


def _ada_kernel(c_ref, w_ref, b_ref, o_ref):
    s = _silu(c_ref[...]).astype(BF16)
    o_ref[0] = _dot(s, w_ref[0].astype(BF16)) + b_ref[0]


def _ada_call(c_all, w_ada, b_ada):
    n = c_all.shape[0]
    width = N_MOD * D_MODEL
    tn = 1536
    return pl.pallas_call(
        _ada_kernel,
        grid=(DEPTH, width // tn),
        in_specs=[
            pl.BlockSpec((n, D_MODEL), lambda l, j: (0, 0)),
            pl.BlockSpec((1, D_MODEL, tn), lambda l, j: (l, 0, j)),
            pl.BlockSpec((1, 1, tn), lambda l, j: (l, 0, j)),
        ],
        out_specs=pl.BlockSpec((1, n, tn), lambda l, j: (l, 0, j)),
        out_shape=jax.ShapeDtypeStruct((DEPTH, n, width), F32),
        compiler_params=_cparams(("arbitrary", "arbitrary")),
        name="ada_mod",
    )(c_all, w_ada, b_ada.reshape(DEPTH, 1, width))


def _mixin_kernel(has_prev, prompt, n_alias, tiles_per_seq, *refs):
    it = iter(refs)
    xin_ref = next(it)
    if has_prev:
        fin_ref, g2_ref = next(it), next(it)
    lng_ref, lnb_ref, sh_ref, sc_ref = next(it), next(it), next(it), next(it)
    wh_ref, wq_ref, wk_ref, wv_ref, wff_ref, wg_ref = (next(it), next(it), next(it), next(it),
                                                       next(it), next(it))
    bff_ref, bfft_ref = next(it), next(it)
    if prompt:
        tril_ref, triu_ref, plq_ref, plk_ref = next(it), next(it), next(it), next(it)
    for _ in range(n_alias):
        next(it)
    x_ref, hq_ref, hf_ref, hi_ref, hg_ref = next(it), next(it), next(it), next(it), next(it)
    fq_ref, ga_ref, gb_ref = next(it), next(it), next(it)
    k_ref, v_ref, lf_ref = next(it), next(it), next(it)
    if prompt:
        ka_ref, va_ref, carry_c, carry_r = next(it), next(it), next(it), next(it)

    x = xin_ref[...]
    if has_prev:
        x = DEEPNORM_ALPHA * x + g2_ref[0, 0] * fin_ref[...]
    x = _layer_norm(x, lng_ref[...], lnb_ref[...])
    x_ref[...] = x
    h = (x * (1.0 + sc_ref[0, 0]) + sh_ref[0, 0]).astype(BF16)
    h32 = h.astype(F32)

    ph = _dot(h, wh_ref[...])
    hq_ref[...] = ph[:, 0 * HG_WIDTH:1 * HG_WIDTH]
    hf_ref[...] = ph[:, 1 * HG_WIDTH:2 * HG_WIDTH]
    hi_ref[...] = ph[:, 2 * HG_WIDTH:3 * HG_WIDTH]
    hg_ref[...] = ph[:, 3 * HG_WIDTH:4 * HG_WIDTH]
    q = _dot(h, wq_ref[...]) * (FOX_HEAD_DIM ** -0.5)
    pg = _dot(h, wg_ref[...])
    ga_ref[...] = pg[:, 0:D_MODEL]
    gb_ref[...] = pg[:, D_MODEL:2 * D_MODEL]

    if not prompt:
        fq_ref[...] = q
        k_ref[...] = _dot(h, wk_ref[...])
        v_ref[...] = _dot(h, wv_ref[...])
        lf_ref[...] = _log_sigmoid(_dot(h32, wff_ref[...]) + bff_ref[...])
    else:
        kvf = _dot(h, wk_ref[...])
        kt = kvf[:, 0:FOX_WIDTH].T
        vt = kvf[:, FOX_WIDTH:2 * FOX_WIDTH].T
        k_ref[0, 0] = kt
        v_ref[0, 0] = vt
        ff = kvf[:, 2 * FOX_WIDTH:2 * FOX_WIDTH + LANES]
        lf = _log_sigmoid(ff[:, 0:FOX_HEADS] + bff_ref[...])
        lft = _log_sigmoid(ff.T[0:FOX_HEADS, :] + bfft_ref[...])
        lf_ref[0, 0] = lft

        @pl.when(pl.program_id(0) % tiles_per_seq == 0)
        def _():
            carry_c[...] = jnp.zeros_like(carry_c)
            carry_r[...] = jnp.zeros_like(carry_r)

        fcol = _dot_sel_lhs(tril_ref[...], lf) + carry_c[...]
        frow = _dot_sel_rhs(lft, triu_ref[...]) + carry_r[...]
        tm = lf.shape[0]
        carry_c[...] = fcol[tm - 1:tm, :]
        carry_r[...] = frow[:, tm - 1:tm]

        low = lax.broadcasted_iota(jnp.int32, (1, LANES), 1) < FOX_HEAD_DIM
        q_aug = _dot(jnp.concatenate(list(_split3(fcol)) + [jnp.ones((tm, FOX_HEADS), F32)], axis=1),
                     plq_ref[...])
        k_aug = _dot(plk_ref[...],
                     jnp.concatenate([-r for r in _split3(frow)] + [jnp.ones((FOX_HEADS, tm), F32)], axis=0))
        ones = jnp.ones((FOX_HEAD_DIM, tm), F32)
        for hd in range(FOX_HEADS):
            feat = slice(hd * FOX_HEAD_DIM, (hd + 1) * FOX_HEAD_DIM)
            blk = slice(hd * LANES, (hd + 1) * LANES)
            src = q[:, (hd // 2) * LANES:(hd // 2 + 1) * LANES]
            if hd % 2 == 0:
                fq_ref[:, blk] = jnp.where(low, src, q_aug[:, blk]).astype(BF16)
                ka_ref[0, blk, :] = jnp.concatenate([kt[feat, :], k_aug[feat, :]], axis=0).astype(BF16)
                va_ref[0, blk, :] = jnp.concatenate([vt[feat, :], ones], axis=0).astype(BF16)
            else:
                fq_ref[:, blk] = jnp.where(low, q_aug[:, blk], src).astype(BF16)
                ka_ref[0, blk, :] = jnp.concatenate([k_aug[feat, :], kt[feat, :]], axis=0).astype(BF16)
                va_ref[0, blk, :] = jnp.concatenate([ones, vt[feat, :]], axis=0).astype(BF16)


def _bias_placements():
    plq = np.zeros((4 * FOX_HEADS, FOX_HEADS * LANES), np.float32)
    plk = np.zeros((FOX_HEADS * FOX_HEAD_DIM, 4 * FOX_HEADS), np.float32)
    for hd in range(FOX_HEADS):
        spare = hd * LANES + (FOX_HEAD_DIM if hd % 2 == 0 else 0)
        for j in range(3):
            plq[j * FOX_HEADS + hd, spare + j] = 1.0
            plq[3 * FOX_HEADS + hd, spare + 3 + j] = 1.0
            plk[hd * FOX_HEAD_DIM + j, 3 * FOX_HEADS + hd] = 1.0
            plk[hd * FOX_HEAD_DIM + 3 + j, j * FOX_HEADS + hd] = 1.0
    return jnp.asarray(plq), jnp.asarray(plk)


def _mixin_call(layer, x_in, f_in, prev_mod, ln_g, ln_b, mod, w, n_seq, seq_len, tm, prompt, kv_bufs):
    T = x_in.shape[0]
    tiles_per_seq = max(seq_len // tm, 1)
    tok = lambda width: pl.BlockSpec((tm, width), lambda i: (i, 0))
    has_prev = f_in is not None

    args, specs = [x_in], [tok(D_MODEL)]
    if has_prev:
        args += [f_in, prev_mod]
        specs += [tok(D_MODEL), _mod_spec(prev_mod, 5, tm, tiles_per_seq)]
    args += [ln_g.reshape(1, D_MODEL), ln_b.reshape(1, D_MODEL), mod, mod]
    specs += [_const_spec((1, D_MODEL)), _const_spec((1, D_MODEL)),
              _mod_spec(mod, 0, tm, tiles_per_seq), _mod_spec(mod, 1, tm, tiles_per_seq)]
    if prompt:
        wnames = ("wh", "wq", "wkvf", "wff", "wff", "wg", "bff", "bfft")
    else:
        wnames = ("wh", "wq", "wk", "wv", "wff", "wg", "bff", "bfft")
    args += [w[n] for n in wnames]
    specs += [_const_spec(w[n].shape) for n in wnames]
    if prompt:
        plq, plk = _bias_placements()
        args += [_tri(tm, True), _tri(tm, False), plq, plk]
        specs += [_const_spec((tm, tm)), _const_spec((tm, tm)), _const_spec(plq.shape), _const_spec(plk.shape)]

    names = ["x", "hq", "hf", "hi", "hg", "fq", "ga", "gb", "k", "v", "lf"]
    widths = [D_MODEL, HG_WIDTH, HG_WIDTH, HG_WIDTH, HG_WIDTH,
              FOX_HEADS * LANES if prompt else FOX_WIDTH, D_MODEL, D_MODEL]
    dtypes = [F32] * 5 + [BF16 if prompt else F32] + [F32] * 2
    out_shape = [jax.ShapeDtypeStruct((T, wd), dt) for wd, dt in zip(widths, dtypes)]
    out_specs = [tok(wd) for wd in widths]
    scratch = []
    aliases = {}
    if prompt:
        fm = lambda rows: pl.BlockSpec((1, 1, rows, tm),
                                       lambda i: (i // tiles_per_seq, layer, 0, i % tiles_per_seq))
        out_shape += [jax.ShapeDtypeStruct((n_seq, DEPTH, FOX_WIDTH, seq_len), F32),
                      jax.ShapeDtypeStruct((n_seq, DEPTH, FOX_WIDTH, seq_len), F32),
                      jax.ShapeDtypeStruct((n_seq, DEPTH, FOX_HEADS, seq_len), F32)]
        out_specs += [fm(FOX_WIDTH), fm(FOX_WIDTH), fm(FOX_HEADS)]
        if kv_bufs is not None:
            for j, buf in enumerate(kv_bufs):
                aliases[len(args)] = 8 + j
                args.append(buf)
                specs.append(pl.BlockSpec(memory_space=pl.ANY))
        names += ["ka", "va"]
        aug = jax.ShapeDtypeStruct((n_seq, FOX_HEADS * LANES, seq_len), BF16)
        aug_spec = pl.BlockSpec((1, FOX_HEADS * LANES, tm),
                                lambda i: (i // tiles_per_seq, 0, i % tiles_per_seq))
        out_shape += [aug, aug]
        out_specs += [aug_spec, aug_spec]
        scratch = [pltpu.VMEM((1, FOX_HEADS), F32), pltpu.VMEM((FOX_HEADS, 1), F32)]
    else:
        out_shape += [jax.ShapeDtypeStruct((T, FOX_WIDTH), F32), jax.ShapeDtypeStruct((T, FOX_WIDTH), F32),
                      jax.ShapeDtypeStruct((T, FOX_HEADS), F32)]
        out_specs += [tok(FOX_WIDTH), tok(FOX_WIDTH), tok(FOX_HEADS)]

    outs = pl.pallas_call(
        functools.partial(_mixin_kernel, has_prev, prompt, len(aliases), tiles_per_seq),
        grid=(T // tm,),
        in_specs=specs,
        out_specs=out_specs,
        out_shape=out_shape,
        scratch_shapes=scratch,
        input_output_aliases=aliases,
        compiler_params=_cparams(("arbitrary",)),
        name="mixer_in",
    )(*args)
    return dict(zip(names, outs))


def _cumsum_rows(x):
    n = x.shape[0]
    row = lax.broadcasted_iota(jnp.int32, x.shape, 0)
    shift = 1
    while shift < n:
        x = x + jnp.where(row >= shift, pltpu.roll(x, shift, 0), 0.0)
        shift *= 2
    return x


def _hgrn_kernel(layer, chunk, sub, n_chunks, hq_ref, hf_ref, hi_ref, hg_ref, lb_ref, ng_ref, s0_ref,
                 o_ref, sout_ref, st_ref):
    t = pl.program_id(1)
    mm = BF16 if chunk >= 16 else F32

    @pl.when(t == 0)
    def _():
        for hd in range(HG_HEADS):
            st_ref[hd] = s0_ref[0, hd].T

    lb_all = lb_ref[...]
    e = jnp.exp(lb_all - jnp.max(lb_all, axis=0, keepdims=True))
    p = e / jnp.sum(e, axis=0, keepdims=True)
    lb = jnp.zeros((1, HG_WIDTH), F32)
    for j in range(1, layer + 1):
        lb = lb + p[j:j + 1, :]
    log_lb = jnp.log(jnp.maximum(lb, LB_FLOOR))
    log1m_lb = jnp.log1p(-lb)
    n_sub = chunk // sub
    causal = (lax.broadcasted_iota(jnp.int32, (chunk, chunk), 1)
              <= lax.broadcasted_iota(jnp.int32, (chunk, chunk), 0))

    def chunk_body(ci, carry):
        r0 = pl.multiple_of(ci * chunk, chunk)
        for hd in range(HG_HEADS):
            ls = slice(hd * HG_DK, (hd + 1) * HG_DK)
            z = hf_ref[pl.ds(r0, chunk), ls]
            q = _silu(hq_ref[pl.ds(r0, chunk), ls])
            v = hi_ref[pl.ds(r0, chunk), ls].astype(mm)
            a = log_lb[:, ls]
            b = log1m_lb[:, ls] + _log_sigmoid(z)
            log_f = jnp.maximum(a, b) + jnp.log1p(jnp.exp(-jnp.abs(a - b)))
            k = (1.0 - lb[:, ls]) * jax.nn.sigmoid(-z)
            cum = _cumsum_rows(log_f)
            blocks = []
            for bi in range(n_sub):
                rows = slice(bi * sub, (bi + 1) * sub)
                base = jnp.zeros((1, HG_DK), F32) if bi == 0 else cum[bi * sub - 1:bi * sub, :]
                k_i = (k * jnp.exp(jnp.minimum(base - cum, EXP_CLAMP))).astype(mm)
                q_i = (q[rows, :] * jnp.exp(cum[rows, :] - base)).astype(mm)
                blocks.append(_dot(q_i, k_i, NT_DIMS))
            attn = blocks[0] if n_sub == 1 else jnp.concatenate(blocks, axis=0)
            attn = jnp.where(causal, attn, 0.0).astype(mm)
            s_t = st_ref[hd]
            o = _dot(attn, v) + _dot((q * jnp.exp(cum)).astype(mm), s_t.astype(mm), NT_DIMS)
            cum_end = cum[chunk - 1:chunk, :]
            k_end = (k * jnp.exp(cum_end - cum)).astype(mm)
            st_ref[hd] = jnp.exp(cum_end) * s_t + _dot(v, k_end, TN_DIMS)
            o = o * lax.rsqrt(jnp.mean(o * o, axis=-1, keepdims=True) + RMS_EPS)
            o_ref[pl.ds(r0, chunk), ls] = o * ng_ref[:, ls] * _silu(hg_ref[pl.ds(r0, chunk), ls])
        return carry

    lax.fori_loop(0, n_chunks, chunk_body, 0, unroll=min(8, n_chunks))

    @pl.when(t == pl.num_programs(1) - 1)
    def _():
        for hd in range(HG_HEADS):
            sout_ref[0, hd] = st_ref[hd].T


def _hgrn_call(layer, p, lower_bounds, norm_g, s0, n_seq, seq_len):
    chunk = HG_CHUNK if seq_len % HG_CHUNK == 0 else seq_len
    sub = min(HG_SUB, chunk)
    tb = min(seq_len, 512)
    nt = seq_len // tb
    tok = pl.BlockSpec((tb, HG_WIDTH), lambda b, t: (b * nt + t, 0))
    st_spec = pl.BlockSpec((1, HG_HEADS, HG_DK, HG_DK), lambda b, t: (b, 0, 0, 0))
    return pl.pallas_call(
        functools.partial(_hgrn_kernel, layer, chunk, sub, tb // chunk),
        grid=(n_seq, nt),
        in_specs=[tok, tok, tok, tok,
                  pl.BlockSpec((DEPTH, HG_WIDTH), lambda b, t: (0, 0)),
                  pl.BlockSpec((1, HG_WIDTH), lambda b, t: (0, 0)),
                  st_spec],
        out_specs=[tok, st_spec],
        out_shape=[jax.ShapeDtypeStruct((n_seq * seq_len, HG_WIDTH), F32),
                   jax.ShapeDtypeStruct((n_seq, HG_HEADS, HG_DK, HG_DK), F32)],
        scratch_shapes=[pltpu.VMEM((HG_HEADS, HG_DK, HG_DK), F32)],
        compiler_params=_cparams(("arbitrary", "arbitrary")),
        name="hgrn",
    )(p["hq"], p["hf"], p["hi"], p["hg"], lower_bounds, norm_g.reshape(1, HG_WIDTH), s0)


def _fox_kernel(tq, tk, qi_ref, ki_ref, q_ref, k_ref, v_ref, o_ref, m_ref, acc_ref):
    qi = qi_ref[pl.program_id(1)]
    ki = ki_ref[pl.program_id(1)]

    @pl.when(ki == 0)
    def _():
        m_ref[...] = jnp.full_like(m_ref, MASK_VALUE)
        acc_ref[...] = jnp.zeros_like(acc_ref)

    def absorb(diagonal):
        if diagonal:
            visible = (lax.broadcasted_iota(jnp.int32, (tq, tk), 1)
                       <= lax.broadcasted_iota(jnp.int32, (tq, tk), 0))
        for hd in range(FOX_HEADS):
            blk = slice(hd * LANES, (hd + 1) * LANES)
            s = _dot(q_ref[:, blk], k_ref[0, blk, :])
            if diagonal:
                s = jnp.where(visible, s, MASK_VALUE)
            cols = [s[:, c * LANES:(c + 1) * LANES] for c in range(tk // LANES)]
            cmax = cols[0]
            for c in cols[1:]:
                cmax = jnp.maximum(cmax, c)
            m_old = m_ref[hd]
            m_new = jnp.maximum(m_old, jnp.broadcast_to(jnp.max(cmax, axis=-1, keepdims=True), (tq, LANES)))
            pe = jnp.concatenate([jnp.exp(c - m_new) for c in cols], axis=1).astype(BF16)
            m_ref[hd] = m_new
            acc_ref[:, blk] = jnp.exp(m_old - m_new) * acc_ref[:, blk] + _dot(pe, v_ref[0, blk, :], NT_DIMS)

    @pl.when(ki < qi)
    def _():
        absorb(False)

    @pl.when(ki == qi)
    def _():
        absorb(True)
        low = lax.broadcasted_iota(jnp.int32, (1, LANES), 1) < FOX_HEAD_DIM
        for pr in range(FOX_HEADS // 2):
            even = acc_ref[:, 2 * pr * LANES:(2 * pr + 1) * LANES]
            odd = acc_ref[:, (2 * pr + 1) * LANES:(2 * pr + 2) * LANES]
            o_ref[:, pr * LANES:(pr + 1) * LANES] = jnp.where(
                low, even / pltpu.roll(even, FOX_HEAD_DIM, 1), odd / pltpu.roll(odd, FOX_HEAD_DIM, 1))


def _fox_call(p, n_seq, seq_len, tq):
    tk = tq
    nq = seq_len // tq
    T = n_seq * seq_len
    width = FOX_HEADS * LANES
    pairs = [(qi, ki) for qi in range(nq) for ki in range(qi + 1)]
    qi_tab = jnp.asarray([pr[0] for pr in pairs], jnp.int32)
    ki_tab = jnp.asarray([pr[1] for pr in pairs], jnp.int32)
    kspec = pl.BlockSpec((1, width, tk), lambda b, s, qt, kt: (b, 0, kt[s]))
    grid_spec = pltpu.PrefetchScalarGridSpec(
        num_scalar_prefetch=2,
        grid=(n_seq, len(pairs)),
        in_specs=[pl.BlockSpec((tq, width), lambda b, s, qt, kt: (b * nq + qt[s], 0)), kspec, kspec],
        out_specs=pl.BlockSpec((tq, FOX_WIDTH), lambda b, s, qt, kt: (b * nq + qt[s], 0)),
        scratch_shapes=[pltpu.VMEM((FOX_HEADS, tq, LANES), F32), pltpu.VMEM((tq, width), F32)],
    )
    return pl.pallas_call(
        functools.partial(_fox_kernel, tq, tk),
        grid_spec=grid_spec,
        out_shape=jax.ShapeDtypeStruct((T, FOX_WIDTH), F32),
        compiler_params=_cparams(("arbitrary", "arbitrary")),
        name="fox_prompt",
    )(qi_tab, ki_tab, p["fq"], p["ka"], p["va"])


PAGES_PER_STEP = 32


def _decode_kernel(n_new, pt_ref, q_ref, kn_ref, vn_ref, lfn_ref, *refs):
    npg = PAGES_PER_STEP
    k_refs = refs[0:npg]
    v_refs = refs[npg:2 * npg]
    lf_refs = refs[2 * npg:3 * npg]
    o_ref, m_ref, l_ref, acc_ref, carry_ref = refs[3 * npg:]
    j = pl.program_id(1)
    rows = FOX_HEADS * n_new

    @pl.when(j == 0)
    def _():
        m_ref[...] = jnp.full_like(m_ref, MASK_VALUE)
        l_ref[...] = jnp.zeros_like(l_ref)
        acc_ref[...] = jnp.zeros_like(acc_ref)
        carry_ref[...] = jnp.zeros_like(carry_ref)

    q = q_ref[...]
    rr = lax.broadcasted_iota(jnp.int32, (rows, FOX_WIDTH), 0)
    cc = lax.broadcasted_iota(jnp.int32, (rows, FOX_WIDTH), 1)
    q_rep = jnp.concatenate([q] * FOX_HEADS, axis=0)
    qbd = jnp.where((rr // n_new) == (cc // FOX_HEAD_DIM), q_rep, 0.0)
    qbd_bf = qbd.astype(BF16)
    ehe = jnp.where(lax.broadcasted_iota(jnp.int32, (rows, FOX_HEADS), 0) // n_new
                    == lax.broadcasted_iota(jnp.int32, (rows, FOX_HEADS), 1), 1.0, 0.0)
    triu = _tri(PAGE_SIZE, False)

    def absorb(s, pv_fn):
        m_old = m_ref[...]
        m_new = jnp.maximum(m_old, jnp.max(s, axis=-1, keepdims=True))
        alpha = jnp.exp(m_old - m_new)
        pe = jnp.exp(s - m_new)
        l_ref[...] = alpha * l_ref[...] + jnp.sum(pe, axis=-1, keepdims=True)
        m_ref[...] = m_new
        acc_ref[...] = alpha * acc_ref[...] + pv_fn(pe)

    lf_all = jnp.concatenate([lf_refs[pg][...] for pg in range(npg)], axis=0)
    within = _dot_sel_rhs(lf_all, triu)
    nr = npg * FOX_HEADS
    ri = lax.broadcasted_iota(jnp.int32, (nr, nr), 0)
    ci = lax.broadcasted_iota(jnp.int32, (nr, nr), 1)
    earlier = jnp.where(jnp.logical_and(ri % FOX_HEADS == ci % FOX_HEADS, ci < ri), 1.0, 0.0)
    totals = jnp.broadcast_to(within[:, PAGE_SIZE - 1:PAGE_SIZE], (nr, PAGE_SIZE))
    carry_rep = jnp.concatenate([jnp.broadcast_to(carry_ref[...], (FOX_HEADS, PAGE_SIZE))] * npg, axis=0)
    f_all = within + _dot_sel_lhs(earlier, totals) + carry_rep
    carry_ref[...] = f_all[nr - FOX_HEADS:nr, PAGE_SIZE - 1:PAGE_SIZE]
    bias = jnp.concatenate(
        [jnp.concatenate([jnp.broadcast_to(f_all[pg * FOX_HEADS + hd:pg * FOX_HEADS + hd + 1, :],
                                           (n_new, PAGE_SIZE)) for hd in range(FOX_HEADS)], axis=0)
         for pg in range(npg)], axis=1)
    kt = jnp.concatenate([k_refs[pg][...].astype(BF16) for pg in range(npg)], axis=1)
    vt = jnp.concatenate([v_refs[pg][...].astype(BF16) for pg in range(npg)], axis=1)
    absorb(_dot(qbd_bf, kt) - bias, lambda pe: _dot(pe.astype(BF16), vt, NT_DIMS))

    @pl.when(j == pl.num_programs(1) - 1)
    def _():
        kn = kn_ref[...]
        vn = vn_ref[...]
        eye = (lax.broadcasted_iota(jnp.int32, (FOX_HEADS, FOX_HEADS), 0)
               == lax.broadcasted_iota(jnp.int32, (FOX_HEADS, FOX_HEADS), 1))
        carry_row = jnp.sum(jnp.where(eye, carry_ref[...], 0.0), axis=0, keepdims=True)
        f_new = _dot_sel_lhs(_tri(n_new, True), lfn_ref[...]) + carry_row
        s = _dot(qbd, kn, NT_DIMS) - _dot_sel_lhs(ehe, f_new, NT_DIMS)
        key_i = lax.broadcasted_iota(jnp.int32, (rows, n_new), 1)
        qry_i = lax.broadcasted_iota(jnp.int32, (rows, n_new), 0) % n_new
        s = jnp.where(key_i <= qry_i, s, MASK_VALUE)
        absorb(s, lambda pe: _dot(pe, vn))
        out = acc_ref[...] / l_ref[...]
        lane_head = lax.broadcasted_iota(jnp.int32, (n_new, FOX_WIDTH), 1) // FOX_HEAD_DIM
        res = jnp.zeros((n_new, FOX_WIDTH), F32)
        for hd in range(FOX_HEADS):
            res = res + jnp.where(lane_head == hd, out[hd * n_new:(hd + 1) * n_new, :], 0.0)
        o_ref[...] = res


def _decode_call(layer, p, cache_kt, cache_vt, cache_lft, page_table, n_seq, n_new):
    n_pages = page_table.shape[1]
    npg = PAGES_PER_STEP
    pt_flat = page_table.reshape(-1).astype(jnp.int32)

    def page_map(pg):
        return lambda n, j, pt: (pt[n * n_pages + j * npg + pg], layer, 0, 0)

    new_w = pl.BlockSpec((n_new, FOX_WIDTH), lambda n, j, pt: (n, 0))
    in_specs = [new_w, new_w, new_w, pl.BlockSpec((n_new, FOX_HEADS), lambda n, j, pt: (n, 0))]
    in_specs += [pl.BlockSpec((None, None, FOX_WIDTH, PAGE_SIZE), page_map(pg)) for pg in range(npg)]
    in_specs += [pl.BlockSpec((None, None, FOX_WIDTH, PAGE_SIZE), page_map(pg)) for pg in range(npg)]
    in_specs += [pl.BlockSpec((None, None, FOX_HEADS, PAGE_SIZE), page_map(pg)) for pg in range(npg)]
    rows = FOX_HEADS * n_new
    grid_spec = pltpu.PrefetchScalarGridSpec(
        num_scalar_prefetch=1,
        grid=(n_seq, n_pages // npg),
        in_specs=in_specs,
        out_specs=new_w,
        scratch_shapes=[pltpu.VMEM((rows, 1), F32), pltpu.VMEM((rows, 1), F32),
                        pltpu.VMEM((rows, FOX_WIDTH), F32), pltpu.VMEM((FOX_HEADS, 1), F32)],
    )
    return pl.pallas_call(
        functools.partial(_decode_kernel, n_new),
        grid_spec=grid_spec,
        out_shape=jax.ShapeDtypeStruct((n_seq * n_new, FOX_WIDTH), F32),
        compiler_params=_cparams(("arbitrary", "arbitrary")),
        name="fox_decode",
    )(pt_flat, p["fq"], p["k"], p["v"], p["lf"],
      *([cache_kt] * npg), *([cache_vt] * npg), *([cache_lft] * npg))


def _merge_kernel(tiles_per_win, part, oa_ref, ob_ref, ga_ref, gb_ref, x_ref, g1_ref, sh_ref, sc_ref,
                  wa_ref, wb_ref, wo_ref, lng_ref, lnb_ref, wrt_ref, br_ref, triu_ref,
                  x1_ref, h2_ref, cw_ref, route_ref, carry_ref):
    @pl.when(pl.program_id(0) % tiles_per_win == 0)
    def _():
        carry_ref[...] = jnp.zeros_like(carry_ref)

    for pi in range(x_ref.shape[0] // part):
        _merge_part(slice(pi * part, (pi + 1) * part), oa_ref, ob_ref, ga_ref, gb_ref, x_ref, g1_ref,
                    sh_ref, sc_ref, wa_ref, wb_ref, wo_ref, lng_ref, lnb_ref, wrt_ref, br_ref, triu_ref,
                    x1_ref, h2_ref, cw_ref, route_ref, carry_ref)


def _merge_part(rs, oa_ref, ob_ref, ga_ref, gb_ref, x_ref, g1_ref, sh_ref, sc_ref,
                wa_ref, wb_ref, wo_ref, lng_ref, lnb_ref, wrt_ref, br_ref, triu_ref,
                x1_ref, h2_ref, cw_ref, route_ref, carry_ref):
    def mod_rows(ref):
        return ref[0, 0] if ref.shape[2] == 1 else ref[0, 0, rs, :]

    ya = _dot(oa_ref[rs, :].astype(BF16), wa_ref[...])
    yb = _dot(ob_ref[rs, :].astype(BF16), wb_ref[...])
    merged = jax.nn.sigmoid(ga_ref[rs, :]) * ya + jax.nn.sigmoid(gb_ref[rs, :]) * yb
    m = _dot(merged.astype(BF16), wo_ref[...])
    x1 = _layer_norm(DEEPNORM_ALPHA * x_ref[rs, :] + mod_rows(g1_ref) * m, lng_ref[...], lnb_ref[...])
    x1_ref[rs, :] = x1
    h2 = x1 * (1.0 + mod_rows(sc_ref)) + mod_rows(sh_ref)
    h2_ref[rs, :] = h2.astype(BF16)

    tm = h2.shape[0]
    hp = [t.astype(BF16) for t in _split3(h2)]
    wr = [wrt_ref[j] for j in range(3)]
    small = _dot(hp[1], wr[1]) + _dot(hp[0], wr[2]) + _dot(hp[2], wr[0])
    mid = _dot(hp[0], wr[1]) + _dot(hp[1], wr[0])
    logits = (small + mid + _dot(hp[0], wr[0])).T[:N_EXPERTS, :] + br_ref[...]
    ex = jnp.exp(logits - jnp.max(logits, axis=0, keepdims=True))
    scores = ex / jnp.sum(ex, axis=0, keepdims=True)
    gs = []
    for g in range(N_GROUPS):
        r = [scores[g * EXPERTS_PER_GROUP + e:g * EXPERTS_PER_GROUP + e + 1, :]
             for e in range(EXPERTS_PER_GROUP)]
        best = r[0] + r[1]
        for a in range(EXPERTS_PER_GROUP):
            for b in range(a + 1, EXPERTS_PER_GROUP):
                best = jnp.maximum(best, r[a] + r[b])
        gs.append(best)
    gmax = jnp.maximum(jnp.maximum(gs[0], gs[1]), jnp.maximum(gs[2], gs[3]))
    gid = jnp.where(gs[0] == gmax, 0, jnp.where(gs[1] == gmax, 1, jnp.where(gs[2] == gmax, 2, 3)))
    erow = lax.broadcasted_iota(jnp.int32, (N_EXPERTS, tm), 0)
    masked = jnp.where(erow // EXPERTS_PER_GROUP == gid, scores, -1.0)
    top1 = jnp.max(masked, axis=0, keepdims=True)
    idx1 = jnp.min(jnp.where(masked == top1, erow, N_EXPERTS), axis=0, keepdims=True)
    masked2 = jnp.where(erow == idx1, -2.0, masked)
    top2 = jnp.max(masked2, axis=0, keepdims=True)
    idx2 = jnp.min(jnp.where(masked2 == top2, erow, N_EXPERTS), axis=0, keepdims=True)
    den = top1 + top2
    cw_ref[:, rs] = jnp.where(erow == idx1, top1 / den, 0.0) + jnp.where(erow == idx2, top2 / den, 0.0)

    grow = lax.broadcasted_iota(jnp.int32, (8, tm), 0)
    member = grow == gid
    incl = _dot(jnp.where(member, 1.0, 0.0), triu_ref[...]) + carry_ref[...]
    carry_ref[...] = incl[:, tm - 1:tm]
    rank = jnp.sum(jnp.where(member, incl, 0.0), axis=0, keepdims=True) - 1.0
    route_ref[:, rs] = jnp.where(grow == 0, gid, jnp.where(grow == 1, rank.astype(jnp.int32), 0))


def _merge_call(o_a, o_b, p, mod, w, ln_g, ln_b, w_router_t, b_router, seq_len, tm, win):
    T = o_a.shape[0]
    part = min(tm, 256)
    tiles_per_seq = max(seq_len // tm, 1)
    tok = lambda width: pl.BlockSpec((tm, width), lambda i: (i, 0))
    return pl.pallas_call(
        functools.partial(_merge_kernel, win // tm, part),
        grid=(T // tm,),
        in_specs=[tok(HG_WIDTH), tok(FOX_WIDTH), tok(D_MODEL), tok(D_MODEL), tok(D_MODEL),
                  _mod_spec(mod, 2, tm, tiles_per_seq), _mod_spec(mod, 3, tm, tiles_per_seq),
                  _mod_spec(mod, 4, tm, tiles_per_seq),
                  _const_spec((HG_WIDTH, D_MODEL)), _const_spec((FOX_WIDTH, D_MODEL)),
                  _const_spec((D_MODEL, D_MODEL)), _const_spec((1, D_MODEL)), _const_spec((1, D_MODEL)),
                  _const_spec((3, D_MODEL, LANES)), _const_spec((N_EXPERTS, 1)), _const_spec((part, part))],
        out_specs=[tok(D_MODEL), tok(D_MODEL),
                   pl.BlockSpec((N_EXPERTS, tm), lambda i: (0, i)),
                   pl.BlockSpec((8, tm), lambda i: (0, i))],
        out_shape=[jax.ShapeDtypeStruct((T, D_MODEL), F32), jax.ShapeDtypeStruct((T, D_MODEL), BF16),
                   jax.ShapeDtypeStruct((N_EXPERTS, T), F32), jax.ShapeDtypeStruct((8, T), jnp.int32)],
        scratch_shapes=[pltpu.VMEM((8, 1), F32)],
        compiler_params=_cparams(("arbitrary",)),
        name="merge",
    )(o_a, o_b, p["ga"], p["gb"], p["x"], mod, mod, mod, w["wa"], w["wb"], w["wo"],
      ln_g.reshape(1, D_MODEL), ln_b.reshape(1, D_MODEL), w_router_t, b_router.reshape(N_EXPERTS, 1),
      _tri(part, False))


MOE_UNIT = 128
MOE_MAX_UNITS = 4


def _moe_kernel(h_ref, cw_ref, route_ref, wg_ref, wu_ref, wd_ref, y_ref):
    g = pl.program_id(1)
    win = h_ref.shape[0]

    @pl.when(g == 0)
    def _():
        y_ref[...] = jnp.zeros_like(y_ref)

    gid = route_ref[0:1, :]
    rank = route_ref[1:2, :]
    member = gid == g
    count = jnp.sum(jnp.where(member, 1, 0))

    def process(base, rows):
        slot = lax.broadcasted_iota(jnp.int32, (rows, win), 0)
        ecol = lax.broadcasted_iota(jnp.int32, (rows, N_EXPERTS), 1)
        perm_f = jnp.where(jnp.logical_and(member, rank - base == slot), 1.0, 0.0)
        perm = perm_f.astype(BF16)
        hs = _dot(perm, h_ref[...]).astype(BF16)
        cwg = _dot_sel_lhs(perm_f, cw_ref[...], NT_DIMS)
        acc = jnp.zeros((rows, D_MODEL), F32)
        for e in range(EXPERTS_PER_GROUP):
            cwe = jnp.sum(jnp.where(ecol == g * EXPERTS_PER_GROUP + e, cwg, 0.0), axis=1, keepdims=True)
            a = _dot(hs, wg_ref[0, e])
            u = _dot(hs, wu_ref[0, e])
            hid = (_silu(a) * u * cwe).astype(BF16)
            acc = acc + _dot(hid, wd_ref[0, e])
        y_ref[...] += _dot(perm, acc.astype(BF16), TN_DIMS)

    big = MOE_UNIT * MOE_MAX_UNITS
    n_big = count // big

    def big_pass(j, carry):
        process(j * big, big)
        return carry

    lax.fori_loop(0, n_big, big_pass, 0)
    rest_units = (count - n_big * big + MOE_UNIT - 1) // MOE_UNIT
    for units in range(1, MOE_MAX_UNITS + 1):
        @pl.when(rest_units == units)
        def _(units=units):
            process(n_big * big, units * MOE_UNIT)


def _moe_call(h2, cw, route, w, win):
    T = h2.shape[0]
    wspec = lambda shp: pl.BlockSpec((1,) + shp, lambda i, g: (g, 0, 0, 0))
    return pl.pallas_call(
        _moe_kernel,
        grid=(T // win, N_GROUPS),
        in_specs=[pl.BlockSpec((win, D_MODEL), lambda i, g: (i, 0)),
                  pl.BlockSpec((N_EXPERTS, win), lambda i, g: (0, i)),
                  pl.BlockSpec((8, win), lambda i, g: (0, i)),
                  wspec((EXPERTS_PER_GROUP, D_MODEL, D_EXPERT)),
                  wspec((EXPERTS_PER_GROUP, D_MODEL, D_EXPERT)),
                  wspec((EXPERTS_PER_GROUP, D_EXPERT, D_MODEL))],
        out_specs=pl.BlockSpec((win, D_MODEL), lambda i, g: (i, 0)),
        out_shape=jax.ShapeDtypeStruct((T, D_MODEL), F32),
        compiler_params=_cparams(("arbitrary", "arbitrary")),
        name="moe",
    )(h2, cw, route, w["eg"], w["eu"], w["ed"])


def _final_kernel(x_ref, f_ref, g2_ref, lng_ref, lnb_ref, o_ref):
    o_ref[...] = _layer_norm(DEEPNORM_ALPHA * x_ref[...] + g2_ref[0, 0] * f_ref[...],
                             lng_ref[...], lnb_ref[...])


def _final_call(x1, f, mod, ln_g, ln_b, seq_len, tm):
    T = x1.shape[0]
    tok = pl.BlockSpec((tm, D_MODEL), lambda i: (i, 0))
    return pl.pallas_call(
        _final_kernel,
        grid=(T // tm,),
        in_specs=[tok, tok, _mod_spec(mod, 5, tm, max(seq_len // tm, 1)),
                  _const_spec((1, D_MODEL)), _const_spec((1, D_MODEL))],
        out_specs=tok,
        out_shape=jax.ShapeDtypeStruct((T, D_MODEL), F32),
        compiler_params=_cparams(("arbitrary",)),
        name="final_norm",
    )(x1, f, mod, ln_g.reshape(1, D_MODEL), ln_b.reshape(1, D_MODEL))


def _layer_weights(prm, l):
    c0 = 4 * HG_WIDTH
    c1 = c0 + FOX_WIDTH
    c2 = c1 + FOX_WIDTH
    c3 = c2 + FOX_WIDTH
    c4 = c3 + FOX_HEADS
    w = prm["w_in"][l]
    grouped = lambda a, shp: a.astype(BF16).reshape((N_GROUPS, EXPERTS_PER_GROUP) + shp)
    return {
        "wh": w[:, :c0].astype(BF16),
        "wq": w[:, c0:c1].astype(BF16),
        "wk": w[:, c1:c2].astype(BF16),
        "wv": w[:, c2:c3].astype(BF16),
        "wff": w[:, c3:c4].astype(BF16).astype(F32),
        "wkvf": jnp.concatenate([w[:, c1:c4].astype(BF16),
                                 jnp.zeros((D_MODEL, LANES - FOX_HEADS), BF16)], axis=1),
        "wg": w[:, c4:].astype(BF16),
        "bff": prm["b_fox_f"][l].reshape(1, FOX_HEADS),
        "bfft": prm["b_fox_f"][l].reshape(FOX_HEADS, 1),
        "wa": prm["w_branch_a"][l].astype(BF16),
        "wb": prm["w_branch_b"][l].astype(BF16),
        "wo": prm["w_out"][l].astype(BF16),
        "eg": grouped(prm["w_exp_gate"][l], (D_MODEL, D_EXPERT)),
        "eu": grouped(prm["w_exp_up"][l], (D_MODEL, D_EXPERT)),
        "ed": grouped(prm["w_exp_down"][l], (D_EXPERT, D_MODEL)),
    }


def _trunk(x, mods, hg_state, paged, prm, weights):
    n_seq, seq_len, _ = x.shape
    T = n_seq * seq_len
    prompt = paged is None
    tm = min(256, T)
    win = min(1024, T)
    wr = jnp.pad(prm["w_router"], ((0, 0), (0, LANES - N_EXPERTS)))
    wr1 = wr.astype(BF16)
    wr2 = (wr - wr1.astype(F32)).astype(BF16)
    wr3 = (wr - wr1.astype(F32) - wr2.astype(F32)).astype(BF16)
    w_router_t = jnp.stack([wr1, wr2, wr3])
    if prompt:
        s0_all = jnp.zeros((DEPTH, n_seq, HG_HEADS, HG_DK, HG_DK), F32)
    else:
        s0_all = hg_state.astype(F32)
        cache_k, cache_v, cache_lf, page_table = paged
        n_phys = cache_k.shape[0]
        cache_kt = jnp.transpose(cache_k, (0, 2, 3, 4, 1)).reshape(n_phys, DEPTH, FOX_WIDTH, PAGE_SIZE)
        cache_vt = jnp.transpose(cache_v, (0, 2, 3, 4, 1)).reshape(n_phys, DEPTH, FOX_WIDTH, PAGE_SIZE)
        cache_lft = jnp.transpose(cache_lf, (0, 2, 3, 1))

    ks, vs, lfs, states = [], [], [], []
    kv_bufs = None
    x_in, f_in = x.reshape(T, D_MODEL), None
    for l in range(DEPTH):
        w = weights[l]
        if l == 0:
            ln_g, ln_b, prev_mod = prm["ln_in_g"], prm["ln_in_b"], None
        else:
            ln_g, ln_b, prev_mod = prm["ln2_g"][l - 1], prm["ln2_b"][l - 1], mods[l - 1]
        p = _mixin_call(l, x_in, f_in, prev_mod, ln_g, ln_b, mods[l], w, n_seq, seq_len, tm, prompt, kv_bufs)
        o_a, s_new = _hgrn_call(l, p, prm["hgrn_lower_bounds"], prm["hgrn_norm_g"][l], s0_all[l],
                                n_seq, seq_len)
        if prompt:
            kv_bufs = (p["k"], p["v"], p["lf"])
            o_b = _fox_call(p, n_seq, seq_len, min(512, seq_len))
        else:
            o_b = _decode_call(l, p, cache_kt, cache_vt, cache_lft, page_table, n_seq, seq_len)
            ks.append(p["k"])
            vs.append(p["v"])
            lfs.append(p["lf"])
        x1, h2, cw, route = _merge_call(o_a, o_b, p, mods[l], w, prm["ln1_g"][l], prm["ln1_b"][l],
                                        w_router_t, prm["b_router"], seq_len, min(512, T), win)
        f = _moe_call(h2, cw, route, w, win)
        x_in, f_in = x1, f
        states.append(s_new)
    y = _final_call(x_in, f_in, mods[DEPTH - 1], prm["ln2_g"][DEPTH - 1], prm["ln2_b"][DEPTH - 1], seq_len, tm)
    y = y.reshape(n_seq, seq_len, D_MODEL)
    if prompt:
        kb, vb, lfb = kv_bufs
        k_out = jnp.transpose(kb.reshape(n_seq, DEPTH, FOX_HEADS, FOX_HEAD_DIM, seq_len), (0, 4, 1, 2, 3))
        v_out = jnp.transpose(vb.reshape(n_seq, DEPTH, FOX_HEADS, FOX_HEAD_DIM, seq_len), (0, 4, 1, 2, 3))
        lf_out = jnp.transpose(lfb, (0, 3, 1, 2))
    else:
        k_out = jnp.stack(ks, axis=1).reshape(n_seq, seq_len, DEPTH, FOX_HEADS, FOX_HEAD_DIM)
        v_out = jnp.stack(vs, axis=1).reshape(n_seq, seq_len, DEPTH, FOX_HEADS, FOX_HEAD_DIM)
        lf_out = jnp.stack(lfs, axis=1).reshape(n_seq, seq_len, DEPTH, FOX_HEADS)
    return y, k_out, v_out, lf_out, jnp.stack(states, axis=0)


def kernel(x_prompt, x_sample, c_prompt, c_sample, cache_k, cache_v, cache_logf, state_hgrn, page_table,
           ln_in_g, ln_in_b, w_ada, b_ada, w_in, b_fox_f, hgrn_lower_bounds, hgrn_norm_g,
           w_branch_a, w_branch_b, w_out, ln1_g, ln1_b, w_router, b_router,
           w_exp_gate, w_exp_up, w_exp_down, ln2_g, ln2_b):
    prm = dict(ln_in_g=ln_in_g, ln_in_b=ln_in_b, w_in=w_in, b_fox_f=b_fox_f,
               hgrn_lower_bounds=hgrn_lower_bounds, hgrn_norm_g=hgrn_norm_g, w_branch_a=w_branch_a,
               w_branch_b=w_branch_b, w_out=w_out, ln1_g=ln1_g, ln1_b=ln1_b, w_router=w_router,
               b_router=b_router, w_exp_gate=w_exp_gate, w_exp_up=w_exp_up, w_exp_down=w_exp_down,
               ln2_g=ln2_g, ln2_b=ln2_b)
    n_p, n_s = x_prompt.shape[0], x_sample.shape[0]
    dec_seq = x_sample.shape[1]
    mod_all = _ada_call(jnp.concatenate([c_prompt, c_sample], axis=0), w_ada, b_ada)
    mods_p, mods_s = [], []
    for l in range(DEPTH):
        mp = mod_all[l, :n_p].reshape(n_p, N_MOD, D_MODEL).transpose(1, 0, 2)
        mods_p.append(mp[:, :, None, :])
        ms = mod_all[l, n_p:].reshape(n_s, N_MOD, D_MODEL).transpose(1, 0, 2)
        mods_s.append(jnp.repeat(ms, dec_seq, axis=1)[:, None, :, :])
    weights = [_layer_weights(prm, l) for l in range(DEPTH)]

    y_p, k_p, v_p, lf_p, hg_p = _trunk(x_prompt, mods_p, None, None, prm, weights)
    y_s, k_s, v_s, lf_s, hg_s = _trunk(x_sample, mods_s, state_hgrn,
                                       (cache_k, cache_v, cache_logf, page_table), prm, weights)
    return (y_p, y_s, k_p, v_p, lf_p, hg_p.astype(x_prompt.dtype),
            k_s, v_s, lf_s, hg_s.astype(state_hgrn.dtype))
```
